```python
import functools
import jax
import jax.numpy as jnp
from jax import lax
import numpy as np

D_MODEL = 1024
BATCH = 2
SEQ = 8192
DEPTH = 2
DEC_BATCH = 32
DEC_SEQ = 4
PAST_LEN = 16384
PAGE_SIZE = 128

H_A = 8
DH_A = 64
W_A = H_A * DH_A
Q_BLOCK = 128
C_B = 512
CONV_W = 31
H_C = 8
DH_C = 64
W_C = H_C * DH_C
LORA_W = 64
LORA_A = 64
LORA_G = 128
C_SHIFT = 3 * W_C + LORA_W + LORA_A + LORA_G
N_BRANCH = 3
D_FF = 2816
FFN_CONV_W = 3
P_COLS = 3 * W_A + H_A + 2 * C_B + C_SHIFT + N_BRANCH * D_MODEL
RMS_EPS = 1e-6
LN_EPS = 1e-5
GN_EPS = 64e-5
L2_EPS = 1e-12
F32 = jnp.float32

kernel_name = 'hybrid_fox_conformer_rwkv7_step'


def split_points(sizes):
    return [int(s) for s in np.cumsum(sizes)[:-1]]


def rmsnorm(x, g):
    xf = x.astype(F32)
    y = xf * lax.rsqrt(jnp.mean(xf * xf, axis=-1, keepdims=True) + RMS_EPS)
    return (y * g.astype(F32)).astype(x.dtype)


def layernorm(x, g, b, eps):
    xf = x.astype(F32)
    xc = xf - jnp.mean(xf, axis=-1, keepdims=True)
    var = jnp.mean(xc * xc, axis=-1, keepdims=True)
    return (xc * lax.rsqrt(var + eps) * g.astype(F32) + b.astype(F32)).astype(x.dtype)


def causal_dwconv(u, buf, w):
    full = jnp.concatenate([buf.astype(u.dtype), u], axis=1)
    out = lax.conv_general_dilated(full, w[:, None, :].astype(u.dtype), window_strides=(1,), padding='VALID',
                                   dimension_numbers=('NWC', 'WIO', 'NWC'), feature_group_count=u.shape[-1])
    return out, full[:, full.shape[1] - (w.shape[0] - 1):]


def token_shift(p, buf, mu):
    prev = jnp.concatenate([buf.astype(p.dtype), p[:, :-1]], axis=1)
    return p + (prev - p) * mu, p[:, -1:]


def fox_prompt(q, k, v, logf):
    B, T, H, D = q.shape
    scale = D ** -0.5
    F = jnp.cumsum(logf.astype(F32), axis=1).transpose(0, 2, 1)
    nb = T // Q_BLOCK
    q_blocks = q.reshape(B, nb, Q_BLOCK, H, D).transpose(1, 0, 2, 3, 4)
    fq_blocks = F.reshape(B, H, nb, Q_BLOCK).transpose(2, 0, 1, 3)
    key_pos = jnp.arange(T)

    def one_block(args):
        blk, qb, fqb = args
        s = jnp.einsum('bqhd,bkhd->bhqk', qb, k, preferred_element_type=F32) * scale
        s = s + (fqb[..., :, None] - F[:, :, None, :])
        q_pos = blk * Q_BLOCK + jnp.arange(Q_BLOCK)
        s = jnp.where(key_pos[None, :] <= q_pos[:, None], s, -jnp.inf)
        p = jax.nn.softmax(s, axis=-1)
        return jnp.einsum('bhqk,bkhd->bqhd', p.astype(v.dtype), v)

    o = lax.map(one_block, (jnp.arange(nb), q_blocks, fq_blocks))
    return o.transpose(1, 0, 2, 3, 4).reshape(B, T, H * D)


def fox_sample(q, k, v, logf, k_pool, v_pool, logf_pool, page_table):
    B, T, H, D = q.shape
    scale = D ** -0.5
    k_past = k_pool[page_table].reshape(B, -1, H, D)
    v_past = v_pool[page_table].reshape(B, -1, H, D)
    l_past = logf_pool[page_table].reshape(B, -1, H)
    P = k_past.shape[1]
    k_all = jnp.concatenate([k_past.astype(k.dtype), k], axis=1)
    v_all = jnp.concatenate([v_past.astype(v.dtype), v], axis=1)
    l_all = jnp.concatenate([l_past.astype(F32), logf.astype(F32)], axis=1)
    F = jnp.cumsum(l_all, axis=1).transpose(0, 2, 1)
    s = jnp.einsum('bqhd,bkhd->bhqk', q, k_all, preferred_element_type=F32) * scale
    s = s + (F[:, :, P:, None] - F[:, :, None, :])
    mask = jnp.arange(P + T)[None, :] <= (P + jnp.arange(T))[:, None]
    p = jax.nn.softmax(jnp.where(mask, s, -jnp.inf), axis=-1)
    o = jnp.einsum('bhqk,bkhd->bqhd', p.astype(v_all.dtype), v_all)
    return o.reshape(B, T, H * D)


def rwkv7_scan(S0, r, w, k, v, kk, a):
    def step(S, inp):
        r_t, w_t, k_t, v_t, kk_t, a_t = inp
        sa = jnp.einsum('bhvk,bhk->bhv', S, kk_t)
        S = S * w_t[:, :, None, :] - sa[..., None] * (kk_t * a_t)[:, :, None, :] + v_t[..., None] * k_t[:, :, None, :]
        return S, jnp.einsum('bhvk,bhk->bhv', S, r_t)

    xs = tuple(jnp.moveaxis(t, 1, 0) for t in (r, w, k, v, kk, a))
    S, ys = lax.scan(step, S0.astype(F32), xs)
    return jnp.moveaxis(ys, 0, 1), S


def rwkv7_mix(p_rwkv, shift_buf, wkv0, prm):
    B, T, _ = p_rwkv.shape
    ps, shift_new = token_shift(p_rwkv, shift_buf, prm['rwkv_mu'])
    r, kc, vc, wd, ad, gd = jnp.split(ps, split_points((W_C, W_C, W_C, LORA_W, LORA_A, LORA_G)), axis=-1)
    w_log = -jax.nn.softplus(-(prm['rwkv_w0'] + jnp.tanh(wd) @ prm['rwkv_w_lora_up']).astype(F32)) - 0.5
    decay = jnp.exp(-jnp.exp(w_log))
    a = jax.nn.sigmoid(prm['rwkv_a0'] + ad @ prm['rwkv_a_lora_up'])
    g = jax.nn.sigmoid(gd) @ prm['rwkv_g_lora_up']

    def heads(t):
        return t.reshape(B, T, H_C, DH_C).astype(F32)

    kk = heads(kc * prm['rwkv_k_k'])
    kk = kk / jnp.maximum(jnp.sqrt(jnp.sum(kk * kk, axis=-1, keepdims=True)), L2_EPS)
    k_eff = heads(kc * (1.0 + (a - 1.0) * prm['rwkv_k_a']))
    r_h, v_h = heads(r), heads(vc)
    y, wkv_new = rwkv7_scan(wkv0, r_h, heads(decay), k_eff, v_h, kk, heads(a))
    y = layernorm(y, prm['rwkv_gn_g'].reshape(H_C, DH_C), prm['rwkv_gn_b'].reshape(H_C, DH_C), GN_EPS)
    bonus = jnp.sum(r_h * k_eff * prm['rwkv_r_k'].astype(F32), axis=-1, keepdims=True) * v_h
    o_c = ((y + bonus).reshape(B, T, W_C) * g.astype(F32)).astype(p_rwkv.dtype)
    return o_c, shift_new, wkv_new


def hybrid_layer(x, prm, attend, conv_buf, shift_buf, wkv0, ffn_buf):
    B, T, _ = x.shape
    h = rmsnorm(x, prm['norm1_g'])
    proj = h @ prm['w_in']
    q, k, v, f_logit, glu_a, glu_b, p_rwkv, gate_logit = jnp.split(
        proj, split_points((W_A, W_A, W_A, H_A, C_B, C_B, C_SHIFT, N_BRANCH * D_MODEL)), axis=-1)
    logf = jax.nn.log_sigmoid((f_logit + prm['b_f']).astype(F32))
    k_h = k.reshape(B, T, H_A, DH_A)
    v_h = v.reshape(B, T, H_A, DH_A)
    o_a = attend(q.reshape(B, T, H_A, DH_A), k_h, v_h, logf)
    u = glu_a * jax.nn.sigmoid(glu_b)
    c, conv_new = causal_dwconv(u, conv_buf, prm['conv_w'])
    o_b = jax.nn.silu(layernorm(c + prm['conv_b'], prm['conv_ln_g'], prm['conv_ln_b'], LN_EPS))
    o_c, shift_new, wkv_new = rwkv7_mix(p_rwkv, shift_buf, wkv0, prm)
    gates = jax.nn.sigmoid(gate_logit + prm['b_gate']).reshape(B, T, N_BRANCH, D_MODEL)
    merged = (gates[:, :, 0] * (o_a @ prm['w_oa']) + gates[:, :, 1] * (o_b @ prm['w_ob'])
              + gates[:, :, 2] * (o_c @ prm['w_oc']))
    x = x + merged @ prm['w_out']
    h2 = rmsnorm(x, prm['norm2_g'])
    ua, ub = jnp.split(h2 @ prm['w_up_ffn'], 2, axis=-1)
    c2, ffn_new = causal_dwconv(ua, ffn_buf, prm['ffn_conv_w'])
    x = x + (jax.nn.gelu(c2) * ub) @ prm['w_down']
    return x, (k_h, v_h, logf, conv_new, shift_new, wkv_new, ffn_new)


def setup_inputs(seed: int = 0) -> dict:
    key = jax.random.key(seed)
    keys = iter(jax.random.split(key, 64))

    def nrm(shape, scale):
        return jax.random.normal(next(keys), shape, F32) * scale

    n_pages = PAST_LEN // PAGE_SIZE
    n_used = DEC_BATCH * n_pages
    n_pool = n_used + (n_used + 3) // 4
    perm = jax.random.permutation(next(keys), n_pool)
    page_table = perm[:n_used].reshape(DEC_BATCH, n_pages).astype(jnp.int32)
    return {
        'x_prompt': nrm((BATCH, SEQ, D_MODEL), 1.0),
        'x_sample': nrm((DEC_BATCH, DEC_SEQ, D_MODEL), 1.0),
        'cache_k': nrm((DEPTH, n_pool, PAGE_SIZE, H_A, DH_A), 1.0),
        'cache_v': nrm((DEPTH, n_pool, PAGE_SIZE, H_A, DH_A), 1.0),
        'cache_logf': jax.nn.log_sigmoid(3.0 + nrm((DEPTH, n_pool, PAGE_SIZE, H_A), 0.5)),
        'state_conv': nrm((DEPTH, DEC_BATCH, CONV_W - 1, C_B), 0.5),
        'state_shift': nrm((DEPTH, DEC_BATCH, 1, C_SHIFT), 1.0),
        'state_wkv': nrm((DEPTH, DEC_BATCH, H_C, DH_C, DH_C), 0.3),
        'state_ffn': nrm((DEPTH, DEC_BATCH, FFN_CONV_W - 1, D_FF), 1.0),
        'page_table': page_table,
        'norm1_g': 1.0 + nrm((DEPTH, D_MODEL), 0.05),
        'w_in': nrm((DEPTH, D_MODEL, P_COLS), D_MODEL ** -0.5),
        'b_f': 3.0 + nrm((DEPTH, H_A), 0.5),
        'b_gate': nrm((DEPTH, N_BRANCH * D_MODEL), 0.02),
        'w_oa': nrm((DEPTH, W_A, D_MODEL), W_A ** -0.5),
        'conv_w': nrm((DEPTH, CONV_W, C_B), CONV_W ** -0.5),
        'conv_b': nrm((DEPTH, C_B), 0.02),
        'conv_ln_g': 1.0 + nrm((DEPTH, C_B), 0.05),
        'conv_ln_b': nrm((DEPTH, C_B), 0.02),
        'w_ob': nrm((DEPTH, C_B, D_MODEL), C_B ** -0.5),
        'rwkv_mu': jax.random.uniform(next(keys), (DEPTH, C_SHIFT), F32),
        'rwkv_w0': -2.5 + nrm((DEPTH, W_C), 1.5),
        'rwkv_w_lora_up': nrm((DEPTH, LORA_W, W_C), 0.1 * LORA_W ** -0.5),
        'rwkv_a0': nrm((DEPTH, W_C), 0.5),
        'rwkv_a_lora_up': nrm((DEPTH, LORA_A, W_C), 0.1 * LORA_A ** -0.5),
        'rwkv_g_lora_up': nrm((DEPTH, LORA_G, W_C), LORA_G ** -0.5),
        'rwkv_k_k': 0.85 + nrm((DEPTH, W_C), 0.05),
        'rwkv_k_a': 1.0 + nrm((DEPTH, W_C), 0.05),
        'rwkv_r_k': nrm((DEPTH, H_C, DH_C), 0.1),
        'rwkv_gn_g': 1.0 + nrm((DEPTH, W_C), 0.05),
        'rwkv_gn_b': nrm((DEPTH, W_C), 0.02),
        'w_oc': nrm((DEPTH, W_C, D_MODEL), W_C ** -0.5),
        'w_out': nrm((DEPTH, D_MODEL, D_MODEL), D_MODEL ** -0.5),
        'norm2_g': 1.0 + nrm((DEPTH, D_MODEL), 0.05),
        'w_up_ffn': nrm((DEPTH, D_MODEL, 2 * D_FF), D_MODEL ** -0.5),
        'ffn_conv_w': nrm((DEPTH, FFN_CONV_W, D_FF), FFN_CONV_W ** -0.5),
        'w_down': nrm((DEPTH, D_FF, D_MODEL), D_FF ** -0.5),
        'norm_f_g': 1.0 + nrm((D_MODEL,), 0.05),
    }


def reference(x_prompt, x_sample, cache_k, cache_v, cache_logf, state_conv, state_shift, state_wkv, state_ffn,
              page_table, norm1_g, w_in, b_f, b_gate, w_oa, conv_w, conv_b, conv_ln_g, conv_ln_b, w_ob,
              rwkv_mu, rwkv_w0, rwkv_w_lora_up, rwkv_a0, rwkv_a_lora_up, rwkv_g_lora_up, rwkv_k_k, rwkv_k_a,
              rwkv_r_k, rwkv_gn_g, rwkv_gn_b, w_oc, w_out, norm2_g, w_up_ffn, ffn_conv_w, w_down, norm_f_g):
    def layer_params(l):
        return {'norm1_g': norm1_g[l], 'w_in': w_in[l], 'b_f': b_f[l], 'b_gate': b_gate[l], 'w_oa': w_oa[l],
                'conv_w': conv_w[l], 'conv_b': conv_b[l], 'conv_ln_g': conv_ln_g[l], 'conv_ln_b': conv_ln_b[l],
                'w_ob': w_ob[l], 'rwkv_mu': rwkv_mu[l], 'rwkv_w0': rwkv_w0[l],
                'rwkv_w_lora_up': rwkv_w_lora_up[l], 'rwkv_a0': rwkv_a0[l], 'rwkv_a_lora_up': rwkv_a_lora_up[l],
                'rwkv_g_lora_up': rwkv_g_lora_up[l], 'rwkv_k_k': rwkv_k_k[l], 'rwkv_k_a': rwkv_k_a[l],
                'rwkv_r_k': rwkv_r_k[l], 'rwkv_gn_g': rwkv_gn_g[l], 'rwkv_gn_b': rwkv_gn_b[l], 'w_oc': w_oc[l],
                'w_out': w_out[l], 'norm2_g': norm2_g[l], 'w_up_ffn': w_up_ffn[l], 'ffn_conv_w': ffn_conv_w[l],
                'w_down': w_down[l]}

    bp = x_prompt.shape[0]
    dt = x_prompt.dtype
    yp = x_prompt
    p_states = []
    for l in range(DEPTH):
        yp, st = hybrid_layer(yp, layer_params(l), fox_prompt,
                              jnp.zeros((bp, CONV_W - 1, C_B), dt), jnp.zeros((bp, 1, C_SHIFT), dt),
                              jnp.zeros((bp, H_C, DH_C, DH_C), F32), jnp.zeros((bp, FFN_CONV_W - 1, D_FF), dt))
        p_states.append(st)
    ys = x_sample
    s_states = []
    for l in range(DEPTH):
        attend = functools.partial(fox_sample, k_pool=cache_k[l], v_pool=cache_v[l], logf_pool=cache_logf[l],
                                   page_table=page_table)
        ys, st = hybrid_layer(ys, layer_params(l), attend, state_conv[l], state_shift[l], state_wkv[l],
                              state_ffn[l])
        s_states.append(st)
    y_prompt = rmsnorm(yp, norm_f_g)
    y_sample = rmsnorm(ys, norm_f_g)

    def stk(states, i):
        return jnp.stack([s[i] for s in states], axis=0)

    return (y_prompt, y_sample,
            stk(p_states, 0), stk(p_states, 1), stk(p_states, 2), stk(p_states, 3),
            stk(p_states, 4), stk(p_states, 5), stk(p_states, 6),
            stk(s_states, 0), stk(s_states, 1), stk(s_states, 2), stk(s_states, 3),
            stk(s_states, 4), stk(s_states, 5), stk(s_states, 6))
```

```python
import functools

import numpy as np
import jax
import jax.numpy as jnp
from jax import lax
from jax.experimental import pallas as pl
from jax.experimental.pallas import tpu as pltpu

F32 = jnp.float32
BF16 = jnp.bfloat16

D_MODEL = 1024
N_HEAD = 8
D_HEAD = 64
W_HEADS = N_HEAD * D_HEAD
LANES = 128
N_PAIR = W_HEADS // LANES
C_SHIFT = 3 * W_HEADS + 64 + 64 + 128
D_FF = 2816
CONV_W = 31
FFN_CONV_W = 3
RMS_EPS = 1e-6
LN_EPS = 1e-5
GN_EPS = 64e-5
L2_EPS = 1e-12
NEG_BIG = -1e30
VMEM_LIMIT = 56 * 2**20


def _params(sem):
    return pltpu.CompilerParams(dimension_semantics=sem, vmem_limit_bytes=VMEM_LIMIT)


def _const_spec(shape):
    nd = len(shape)
    return pl.BlockSpec(shape, lambda *_: (0,) * nd)


def _dot(a, b):
    return jnp.dot(a.astype(BF16), b.astype(BF16), preferred_element_type=F32)


def _dot_nt(a, b):
    return lax.dot_general(a.astype(BF16), b.astype(BF16), (((1,), (1,)), ((), ())),
                           preferred_element_type=F32)


def _split3(x):
    hi = x.astype(BF16).astype(F32)
    r1 = x - hi
    mid = r1.astype(BF16).astype(F32)
    lo = (r1 - mid).astype(BF16).astype(F32)
    return hi, mid, lo


def _tri_dot(tri, x):
    pieces = jnp.concatenate(_split3(x), axis=1).astype(BF16)
    r = jnp.dot(tri, pieces, preferred_element_type=F32)
    return r[:, :LANES] + r[:, LANES:2 * LANES] + r[:, 2 * LANES:]


def _dot_tri(x, tri):
    pieces = jnp.concatenate(_split3(x), axis=0).astype(BF16)
    r = jnp.dot(pieces, tri, preferred_element_type=F32)
    n = x.shape[0]
    return r[:n] + r[n:2 * n] + r[2 * n:]


def _log_sigmoid(z):
    return jnp.minimum(z, 0.0) - jnp.log1p(jnp.exp(-jnp.abs(z)))


def _rmsnorm(x, g):
    return x * lax.rsqrt(jnp.mean(x * x, axis=-1, keepdims=True) + RMS_EPS) * g


def _inproj_body(prompt, x_ref, g_ref, wqkv_ref, wf_ref, bf_ref, wglu_ref, wr_ref, wg_ref, bg_ref, *rest):
    if prompt:
        (eplace_ref, qc_ref, kc_ref, k_ref, v_ref, vb_ref, lf_ref, u_ref, p_ref, gate_ref, fcarry) = rest
    else:
        (q_ref, k_ref, v_ref, lf_ref, u_ref, p_ref, gate_ref) = rest
    tm = x_ref.shape[1]
    hn = _rmsnorm(x_ref[0], g_ref[...]).astype(BF16)

    qkv = jnp.dot(hn, wqkv_ref[...], preferred_element_type=F32)
    q = qkv[:, :W_HEADS] * (D_HEAD ** -0.5)
    k = qkv[:, W_HEADS:2 * W_HEADS]
    v = qkv[:, 2 * W_HEADS:]
    k_ref[0] = k
    v_ref[0] = v

    logf = _log_sigmoid(jnp.dot(hn, wf_ref[...], preferred_element_type=F32) + bf_ref[...])
    lf_ref[0] = logf[:, :N_HEAD]

    glu = jnp.dot(hn, wglu_ref[...], preferred_element_type=F32)
    u_ref[0] = glu[:, :W_HEADS] * jax.nn.sigmoid(glu[:, W_HEADS:])
    p_ref[0] = jnp.dot(hn, wr_ref[...], preferred_element_type=F32)
    gate_ref[0] = jax.nn.sigmoid(jnp.dot(hn, wg_ref[...], preferred_element_type=F32)
                                 + bg_ref[...]).astype(gate_ref.dtype)

    if not prompt:
        q_ref[0] = q.astype(q_ref.dtype)
        return

    vb_ref[0] = v.astype(BF16)

    @pl.when(pl.program_id(1) == 0)
    def _():
        fcarry[...] = jnp.zeros_like(fcarry)

    lane = lax.broadcasted_iota(jnp.int32, (1, LANES), 1)
    lf = jnp.where(lane < N_HEAD, logf, 0.0)
    row = lax.broadcasted_iota(jnp.int32, (tm, tm), 0)
    col = lax.broadcasted_iota(jnp.int32, (tm, tm), 1)
    tri = jnp.where(col <= row, 1.0, 0.0).astype(BF16)
    fcum = _tri_dot(tri, lf) + fcarry[...]
    fcarry[...] = fcum[tm - 1:tm, :]
    aug = jnp.dot(jnp.concatenate(_split3(fcum), axis=1).astype(BF16), eplace_ref[...],
                  preferred_element_type=F32)
    slot = lane & 15
    augq = aug[:, :LANES] + jnp.where((slot >= 3) & (slot < 6), 1.0, 0.0)
    augk = aug[:, LANES:] + jnp.where(slot < 3, 1.0, 0.0)
    augk_b = augk.astype(BF16)
    for pair in range(N_PAIR):
        kc_ref[0, pair] = jnp.concatenate([k[:, pair * LANES:(pair + 1) * LANES].astype(BF16), augk_b], axis=1)
        q_pair = q[:, pair * LANES:(pair + 1) * LANES]
        for par in range(2):
            h = 2 * pair + par
            q_h = jnp.where((lane >= D_HEAD) if par else (lane < D_HEAD), q_pair, 0.0)
            a_h = jnp.where((lane >> 4) == h, augq, 0.0)
            qc_ref[0, h] = jnp.concatenate([q_h, a_h], axis=1).astype(BF16)


def _eplace():
    e = np.zeros((3 * LANES, 2 * LANES), np.float32)
    for piece in range(3):
        for h in range(N_HEAD):
            e[piece * LANES + h, 16 * h + piece] = 1.0
            e[piece * LANES + h, LANES + 16 * h + 3 + piece] = -1.0
    return jnp.asarray(e, BF16)


def _in_proj(x, lw, prompt, tm):
    B, T, _ = x.shape
    nt = T // tm
    row = lambda width: pl.BlockSpec((1, tm, width), lambda b, i: (b, i, 0))
    weights = [lw['norm1_g'], lw['w_qkv'], lw['w_f'], lw['b_f'], lw['w_glu'], lw['w_r'], lw['w_g'], lw['b_gate']]
    in_specs = [row(D_MODEL)] + [_const_spec(w.shape) for w in weights]
    common_shapes = [
        jax.ShapeDtypeStruct((B, T, W_HEADS), F32),
        jax.ShapeDtypeStruct((B, T, W_HEADS), F32),
    ]
    tail_shapes = [
        jax.ShapeDtypeStruct((B, T, N_HEAD), F32),
        jax.ShapeDtypeStruct((B, T, W_HEADS), F32),
        jax.ShapeDtypeStruct((B, T, C_SHIFT), F32),
        jax.ShapeDtypeStruct((B, T, 3 * D_MODEL), BF16),
    ]
    tail_specs = [row(N_HEAD), row(W_HEADS), row(C_SHIFT), row(3 * D_MODEL)]
    if prompt:
        args = [x] + weights + [_eplace()]
        in_specs = in_specs + [_const_spec((3 * LANES, 2 * LANES))]
        out_shape = [jax.ShapeDtypeStruct((B, N_HEAD, T, 2 * LANES), BF16),
                     jax.ShapeDtypeStruct((B, N_PAIR, T, 2 * LANES), BF16)] + common_shapes + \
                    [jax.ShapeDtypeStruct((B, T, W_HEADS), BF16)] + tail_shapes
        out_specs = [pl.BlockSpec((1, N_HEAD, tm, 2 * LANES), lambda b, i: (b, 0, i, 0)),
                     pl.BlockSpec((1, N_PAIR, tm, 2 * LANES), lambda b, i: (b, 0, i, 0)),
                     row(W_HEADS), row(W_HEADS), row(W_HEADS)] + tail_specs
        scratch = [pltpu.VMEM((1, LANES), F32)]
    else:
        args = [x] + weights
        out_shape = [jax.ShapeDtypeStruct((B, T, W_HEADS), BF16)] + common_shapes + tail_shapes
        out_specs = [row(W_HEADS), row(W_HEADS), row(W_HEADS)] + tail_specs
        scratch = []
    return pl.pallas_call(
        functools.partial(_inproj_body, prompt),
        grid=(B, nt), in_specs=in_specs, out_specs=out_specs, out_shape=out_shape,
        scratch_shapes=scratch, compiler_params=_params(("arbitrary", "arbitrary")),
        name="in_proj_prompt" if prompt else "in_proj_sample",
    )(*args)


def _fox_prompt_body(qc_ref, kc_ref, v_ref, o_ref, *, tq):
    i = pl.program_id(2)
    qs = (qc_ref[0, 0], qc_ref[0, 1])
    row = lax.broadcasted_iota(jnp.int32, (tq, tq), 0)
    col = lax.broadcasted_iota(jnp.int32, (tq, tq), 1)

    def tile(j, carry, diagonal):
        start = pl.multiple_of(j * tq, tq)
        kc = kc_ref[0, 0, pl.ds(start, tq), :]
        vv = v_ref[0, pl.ds(start, tq), :]
        out = []
        for q, (m, l, acc) in zip(qs, carry):
            s = lax.dot_general(q, kc, (((1,), (1,)), ((), ())), preferred_element_type=F32)
            if diagonal:
                s = jnp.where(col <= row, s, NEG_BIG)
            m_new = jnp.maximum(m, jnp.max(s, axis=-1, keepdims=True))
            alpha = jnp.exp(m - m_new)
            p = jnp.exp(s - m_new)
            l = alpha * l + jnp.sum(p, axis=-1, keepdims=True)
            acc = alpha * acc + jnp.dot(p.astype(BF16), vv, preferred_element_type=F32)
            out.append((m_new, l, acc))
        return tuple(out)

    init = tuple((jnp.full((tq, 1), NEG_BIG, F32), jnp.zeros((tq, 1), F32), jnp.zeros((tq, LANES), F32))
                 for _ in range(2))
    carry = lax.fori_loop(0, i, lambda j, c: tile(j, c, False), init)
    (_, l0, a0), (_, l1, a1) = tile(i, carry, True)
    lane = lax.broadcasted_iota(jnp.int32, (1, LANES), 1)
    o_ref[0] = jnp.where(lane < D_HEAD, a0 / l0, a1 / l1).astype(o_ref.dtype)


def _fox_prompt(qc, kc, vb, tq):
    B, _, T, _ = qc.shape
    return pl.pallas_call(
        functools.partial(_fox_prompt_body, tq=tq),
        grid=(B, N_PAIR, T // tq),
        in_specs=[pl.BlockSpec((1, 2, tq, 2 * LANES), lambda b, p, i: (b, p, i, 0)),
                  pl.BlockSpec((1, 1, T, 2 * LANES), lambda b, p, i: (b, p, 0, 0)),
                  pl.BlockSpec((1, T, LANES), lambda b, p, i: (b, 0, p))],
        out_specs=pl.BlockSpec((1, tq, LANES), lambda b, p, i: (b, i, p)),
        out_shape=jax.ShapeDtypeStruct((B, T, W_HEADS), BF16),
        compiler_params=_params(("arbitrary", "arbitrary", "arbitrary")),
        name="fox_prompt",
    )(qc, kc, vb)


def _fox_sample_body(n_group, t_new, pt_ref, q_ref, kn_ref, vn_ref, lfn_ref, *rest):
    k_refs = rest[:n_group]
    v_refs = rest[n_group:2 * n_group]
    lf_refs = rest[2 * n_group:3 * n_group]
    o_ref, qbd, m_s, l_s, acc_s, tail_s, newsum_s = rest[3 * n_group:]
    j = pl.program_id(1)
    nrow = t_new * N_HEAD
    page = k_refs[0].shape[0]
    lane_w = lax.broadcasted_iota(jnp.int32, (N_HEAD, W_HEADS), 1)
    head_w = lax.broadcasted_iota(jnp.int32, (N_HEAD, W_HEADS), 0)
    bd_mask = (lane_w >> 6) == head_w
    ri = lax.broadcasted_iota(jnp.int32, (page, page), 0)
    ci = lax.broadcasted_iota(jnp.int32, (page, page), 1)

    def tile4(x):
        return jnp.concatenate([x] * t_new, axis=0)

    def online(s, vv):
        m_new = jnp.maximum(m_s[...], jnp.max(s, axis=-1, keepdims=True))
        alpha = jnp.exp(m_s[...] - m_new)
        p = jnp.exp(s - m_new)
        l_s[...] = alpha * l_s[...] + jnp.sum(p, axis=-1, keepdims=True)
        acc_s[...] = alpha * acc_s[...] + jnp.dot(p.astype(BF16), vv.astype(BF16), preferred_element_type=F32)
        m_s[...] = m_new

    @pl.when(j == 0)
    def _():
        q = q_ref[0].astype(F32)
        qbd[...] = jnp.concatenate(
            [jnp.where(bd_mask, jnp.broadcast_to(q[t:t + 1, :], (N_HEAD, W_HEADS)), 0.0)
             for t in range(t_new)], axis=0).astype(BF16)
        m_s[...] = jnp.full_like(m_s, NEG_BIG)
        l_s[...] = jnp.zeros_like(l_s)
        acc_s[...] = jnp.zeros_like(acc_s)
        tail_s[...] = jnp.zeros_like(tail_s)
        tri_incl = jnp.where(ri <= ci, 1.0, 0.0).astype(BF16)
        cum = _dot_tri(lfn_ref[0], tri_incl)
        newsum = jnp.concatenate([cum[:, t:t + 1] for t in range(t_new)], axis=0)
        newsum_s[...] = newsum
        s = _dot_nt(qbd[...], kn_ref[0])
        rr = lax.broadcasted_iota(jnp.int32, (nrow, page), 0) >> 3
        cc = lax.broadcasted_iota(jnp.int32, (nrow, page), 1)
        s = jnp.where(cc <= rr, s + newsum - tile4(cum), NEG_BIG)
        online(s, vn_ref[0])

    tri_after = jnp.where(ri > ci, 1.0, 0.0).astype(BF16)
    for g in range(n_group):
        lf = lf_refs[g][...]
        bias8 = _dot_tri(lf, tri_after) + tail_s[...]
        s = _dot_nt(qbd[...], k_refs[g][...]) + tile4(bias8) + newsum_s[...]
        online(s, v_refs[g][...])
        tail_s[...] = tail_s[...] + jnp.sum(lf, axis=-1, keepdims=True)

    @pl.when(j == pl.num_programs(1) - 1)
    def _():
        o = acc_s[...] / l_s[...]
        for t in range(t_new):
            o_ref[0, t:t + 1, :] = jnp.sum(jnp.where(bd_mask, o[t * N_HEAD:(t + 1) * N_HEAD, :], 0.0), axis=0,
                                           keepdims=True).astype(o_ref.dtype)


def _fox_sample(q, k_new, v_new, lf_new_t, k_pool, v_pool, lf_pool_t, page_table, n_group):
    B, t_new, _ = q.shape
    page = k_pool.shape[1]
    n_pages = page_table.shape[1]
    steps = n_pages // n_group

    def paged(width_shape):
        specs = []
        for g in range(n_group):
            specs.append(pl.BlockSpec((None,) + width_shape,
                                      lambda b, j, pt, g=g: (pt[b, n_pages - 1 - (j * n_group + g)], 0, 0)))
        return specs

    per_b = lambda shape: pl.BlockSpec((1,) + shape, lambda b, j, pt: (b, 0, 0))
    nrow = t_new * N_HEAD
    grid_spec = pltpu.PrefetchScalarGridSpec(
        num_scalar_prefetch=1, grid=(B, steps),
        in_specs=[per_b((t_new, W_HEADS)), per_b((page, W_HEADS)), per_b((page, W_HEADS)), per_b((N_HEAD, page))]
        + paged((page, W_HEADS)) + paged((page, W_HEADS)) + paged((N_HEAD, page)),
        out_specs=per_b((t_new, W_HEADS)),
        scratch_shapes=[pltpu.VMEM((nrow, W_HEADS), BF16), pltpu.VMEM((nrow, 1), F32), pltpu.VMEM((nrow, 1), F32),
                        pltpu.VMEM((nrow, W_HEADS), F32), pltpu.VMEM((N_HEAD, 1), F32), pltpu.VMEM((nrow, 1), F32)])
    return pl.pallas_call(
        functools.partial(_fox_sample_body, n_group, t_new),
        grid_spec=grid_spec,
        out_shape=jax.ShapeDtypeStruct((B, t_new, W_HEADS), F32),
        compiler_params=_params(("arbitrary", "arbitrary")),
        name="fox_sample",
    )(page_table, q, k_new, v_new, lf_new_t, *([k_pool] * n_group), *([v_pool] * n_group),
      *([lf_pool_t] * n_group))


HIST_ROWS = 32


def _conv_body(u_ref, hist_ref, w_ref, b_ref, g_ref, beta_ref, o_ref, ext):
    tt = u_ref.shape[1]

    @pl.when(pl.program_id(1) == 0)
    def _():
        ext[0:HIST_ROWS, :] = hist_ref[0]

    ext[HIST_ROWS:HIST_ROWS + tt, :] = u_ref[0]
    first = HIST_ROWS - (CONV_W - 1)
    acc = jnp.zeros((tt, W_HEADS), F32)
    for tap in range(CONV_W):
        acc = acc + w_ref[tap:tap + 1, :] * ext[first + tap:first + tap + tt, :]
    c = acc + b_ref[...]
    mean = jnp.mean(c, axis=-1, keepdims=True)
    xc = c - mean
    var = jnp.mean(xc * xc, axis=-1, keepdims=True)
    y = xc * lax.rsqrt(var + LN_EPS) * g_ref[...] + beta_ref[...]
    o_ref[0] = (y * jax.nn.sigmoid(y)).astype(o_ref.dtype)
    ext[0:HIST_ROWS, :] = ext[tt:tt + HIST_ROWS, :]


def _conv(u, hist, lw, tt):
    B, T, _ = u.shape
    return pl.pallas_call(
        _conv_body, grid=(B, T // tt),
        in_specs=[pl.BlockSpec((1, tt, W_HEADS), lambda b, i: (b, i, 0)),
                  pl.BlockSpec((1, HIST_ROWS, W_HEADS), lambda b, i: (b, 0, 0)),
                  _const_spec((CONV_W, W_HEADS)), _const_spec((1, W_HEADS)), _const_spec((1, W_HEADS)),
                  _const_spec((1, W_HEADS))],
        out_specs=pl.BlockSpec((1, tt, W_HEADS), lambda b, i: (b, i, 0)),
        out_shape=jax.ShapeDtypeStruct((B, T, W_HEADS), BF16),
        scratch_shapes=[pltpu.VMEM((HIST_ROWS + tt, W_HEADS), F32)],
        compiler_params=_params(("arbitrary", "arbitrary")),
        name="conv",
    )(u, hist, lw['conv_w'], lw['conv_b'], lw['conv_ln_g'], lw['conv_ln_b'])


def _rwkv_prep_body(p_ref, hist_ref, mu_ref, w0_ref, a0_ref, kk_ref, ka_ref, rk_ref, wwa_ref, wg_ref, bd_ref,
                    r_o, lw_o, k_o, v_o, kk_o, b_o, g_o, bonus_o, ext):
    tt = p_ref.shape[1]

    @pl.when(pl.program_id(1) == 0)
    def _():
        ext[0:8, :] = hist_ref[0]

    p = p_ref[0]
    ext[8:8 + tt, :] = p
    prev = ext[7:7 + tt, :]
    ps = p + (prev - p) * mu_ref[...]
    ext[0:8, :] = ext[tt:tt + 8, :]

    r = ps[:, :W_HEADS]
    kc = ps[:, W_HEADS:2 * W_HEADS]
    vc = ps[:, 2 * W_HEADS:3 * W_HEADS]
    wa = ps[:, 3 * W_HEADS:3 * W_HEADS + LANES]
    gd = ps[:, 3 * W_HEADS + LANES:]
    lane = lax.broadcasted_iota(jnp.int32, (1, LANES), 1)
    wa = jnp.where(lane < 64, jnp.tanh(wa), wa)
    lora = _dot(wa, wwa_ref[...])
    lw_o[0] = -float(np.exp(-0.5)) * jax.nn.sigmoid(w0_ref[...] + lora[:, :W_HEADS])
    a = jax.nn.sigmoid(a0_ref[...] + lora[:, W_HEADS:])
    g_o[0] = _dot(jax.nn.sigmoid(gd), wg_ref[...])
    bd = bd_ref[...]
    kk = kc * kk_ref[...]
    sq = kk * kk
    sq_hi = sq.astype(BF16)
    sq_lo = (sq - sq_hi.astype(F32)).astype(BF16)
    ss = jnp.dot(sq_hi, bd, preferred_element_type=F32) + jnp.dot(sq_lo, bd, preferred_element_type=F32)
    kk = kk / jnp.maximum(jnp.sqrt(ss), L2_EPS)
    k_eff = kc * (1.0 + (a - 1.0) * ka_ref[...])
    r_o[0] = r
    k_o[0] = k_eff
    v_o[0] = vc
    kk_o[0] = kk
    b_o[0] = kk * a
    bonus_o[0] = _dot(r * k_eff * rk_ref[...], bd) * vc


def _rwkv_prep(p, hist, lw, tt):
    B, T, _ = p.shape
    row = lambda width: pl.BlockSpec((1, tt, width), lambda b, i: (b, i, 0))
    consts = [lw['rwkv_mu'], lw['rwkv_w0'], lw['rwkv_a0'], lw['rwkv_k_k'], lw['rwkv_k_a'], lw['rwkv_r_k'],
              lw['w_wa'], lw['rwkv_g_lora_up'], lw['bd_ones']]
    return pl.pallas_call(
        _rwkv_prep_body, grid=(B, T // tt),
        in_specs=[row(C_SHIFT), pl.BlockSpec((1, 8, C_SHIFT), lambda b, i: (b, 0, 0))]
        + [_const_spec(c.shape) for c in consts],
        out_specs=[row(W_HEADS)] * 8,
        out_shape=[jax.ShapeDtypeStruct((B, T, W_HEADS), F32)] * 8,
        scratch_shapes=[pltpu.VMEM((8 + tt, C_SHIFT), F32)],
        compiler_params=_params(("arbitrary", "arbitrary")),
        name="rwkv_prep",
    )(p, hist, *consts)


def _rwkv_scan_body(r_ref, lw_ref, k_ref, v_ref, kk_ref, b_ref, a0_ref, y_ref, a_out_ref, a_s, *, chunk):
    c = chunk
    n = 2 * c

    @pl.when(pl.program_id(1) == 0)
    def _():
        a_s[...] = a0_ref[0]

    ri = lax.broadcasted_iota(jnp.int32, (n, n), 0)
    ci = lax.broadcasted_iota(jnp.int32, (n, n), 1)
    strict = ri > ci
    incl = ri >= ci
    eye = ri == ci
    tr = lax.broadcasted_iota(jnp.int32, (c, c), 0)
    tc = lax.broadcasted_iota(jnp.int32, (c, c), 1)
    tri_incl = jnp.where(tc <= tr, 1.0, 0.0).astype(BF16)
    lane = lax.broadcasted_iota(jnp.int32, (1, LANES), 1)
    m_even = lane < D_HEAD

    def stack(x):
        return jnp.concatenate([jnp.where(m_even, x, 0.0), jnp.where(m_even, 0.0, x)], axis=0)

    for pair in range(N_PAIR):
        sl = slice(pair * LANES, (pair + 1) * LANES)
        lw = lw_ref[0, :, sl]
        cum = _tri_dot(tri_incl, lw)
        cum_prev = cum - lw
        cum_end = cum[c - 1:c, :]
        e_neg = jnp.exp(-cum)
        e_end = jnp.exp(cum_end - cum)
        r = r_ref[0, :, sl]
        k = k_ref[0, :, sl]
        v = v_ref[0, :, sl]
        kk = kk_ref[0, :, sl]
        b = b_ref[0, :, sl]
        rt = stack(r * jnp.exp(cum)).astype(BF16)
        ka = stack(kk * jnp.exp(cum_prev)).astype(BF16)
        kt = stack(k * e_neg).astype(BF16)
        bt = stack(b * e_neg).astype(BF16)
        kh = stack(k * e_end)
        bh = stack(b * e_end)
        vs = stack(v).astype(BF16)

        m_mat = jnp.where(strict, _dot_nt(ka, kt), 0.0)
        l_mat = jnp.where(strict, _dot_nt(ka, bt), 0.0)
        n1 = jnp.where(incl, _dot_nt(rt, kt), 0.0)
        n2 = jnp.where(incl, _dot_nt(rt, bt), 0.0).astype(BF16)

        inv = jnp.where(eye, 1.0, 0.0) - jnp.where((ri >> 1) == (ci >> 1), l_mat, 0.0)
        size = 2
        shift = 1
        while size < c:
            blk = ((ri >> (shift + 1)) == (ci >> (shift + 1))) & (((ri >> shift) & 1) == 1) & (((ci >> shift) & 1) == 0)
            off = jnp.where(blk, l_mat, 0.0)
            inv = inv - _dot(_dot(inv, off), inv)
            size *= 2
            shift += 1
        inv_b = inv.astype(BF16)

        mv = _dot(m_mat, vs)
        wm = _dot(inv_b, ka)
        u0 = _dot(inv_b, mv)
        wm_b = wm.astype(BF16)
        u0_b = u0.astype(BF16)
        rq = rt.astype(F32) - _dot(n2, wm_b)
        y0 = _dot(n1, vs) - _dot(n2, u0_b)
        bh_t = bh.T.astype(BF16)
        kh_t = kh.T.astype(BF16)
        decay_end = jnp.where(lax.broadcasted_iota(jnp.int32, (LANES, LANES), 0)
                              == lax.broadcasted_iota(jnp.int32, (LANES, LANES), 1),
                              jnp.broadcast_to(jnp.exp(cum_end), (LANES, LANES)), 0.0)
        g_mat = decay_end - _dot(bh_t, wm_b)
        h_mat = _dot(kh_t, vs) - _dot(bh_t, u0_b)
        a_prev = a_s[pair]
        ys = _dot(rq, a_prev) + y0
        y_ref[0, :, sl] = ys[:c] + ys[c:]
        a_s[pair] = _dot(g_mat, a_prev) + h_mat

    @pl.when(pl.program_id(1) == pl.num_programs(1) - 1)
    def _():
        a_out_ref[0] = a_s[...]


def _rwkv_scan(r, lwd, k, v, kk, b, a0, chunk):
    B, T, _ = r.shape
    row = pl.BlockSpec((1, chunk, W_HEADS), lambda bb, i: (bb, i, 0))
    st = pl.BlockSpec((1, N_PAIR, LANES, LANES), lambda bb, i: (bb, 0, 0, 0))
    return pl.pallas_call(
        functools.partial(_rwkv_scan_body, chunk=chunk), grid=(B, T // chunk),
        in_specs=[row] * 6 + [st], out_specs=[row, st],
        out_shape=[jax.ShapeDtypeStruct((B, T, W_HEADS), F32),
                   jax.ShapeDtypeStruct((B, N_PAIR, LANES, LANES), F32)],
        scratch_shapes=[pltpu.VMEM((N_PAIR, LANES, LANES), F32)],
        compiler_params=_params(("arbitrary", "arbitrary")),
        name="rwkv_scan",
    )(r, lwd, k, v, kk, b, a0)


def _merge_body(x_ref, oa_ref, ob_ref, y_ref, bonus_ref, g_ref, gate_ref, woa_ref, wob_ref, woc_ref, wout_ref,
                gng_ref, gnb_ref, bd_ref, o_ref):
    bd = bd_ref[...]
    y = y_ref[0]
    mean = _dot(y, bd) * (1.0 / D_HEAD)
    xc = y - mean
    var = _dot(xc * xc, bd) * (1.0 / D_HEAD)
    yn = xc * lax.rsqrt(var + GN_EPS) * gng_ref[...] + gnb_ref[...]
    oc = (yn + bonus_ref[0]) * g_ref[0]
    gates = gate_ref[0].astype(F32)
    merged = (gates[:, :D_MODEL] * _dot(oa_ref[0], woa_ref[...])
              + gates[:, D_MODEL:2 * D_MODEL] * _dot(ob_ref[0], wob_ref[...])
              + gates[:, 2 * D_MODEL:] * _dot(oc, woc_ref[...]))
    o_ref[0] = x_ref[0] + _dot(merged, wout_ref[...])


def _merge(x, oa, ob, y, bonus, g, gates, lw, tm):
    B, T, _ = x.shape
    row = lambda width: pl.BlockSpec((1, tm, width), lambda b, i: (b, i, 0))
    consts = [lw['w_oa'], lw['w_ob'], lw['w_oc'], lw['w_out'], lw['rwkv_gn_g'], lw['rwkv_gn_b'], lw['bd_ones']]
    return pl.pallas_call(
        _merge_body, grid=(B, T // tm),
        in_specs=[row(D_MODEL), row(W_HEADS), row(W_HEADS), row(W_HEADS), row(W_HEADS), row(W_HEADS),
                  row(3 * D_MODEL)] + [_const_spec(c.shape) for c in consts],
        out_specs=row(D_MODEL), out_shape=jax.ShapeDtypeStruct((B, T, D_MODEL), F32),
        compiler_params=_params(("arbitrary", "arbitrary")),
        name="merge",
    )(x, oa, ob, y, bonus, g, gates, *consts)


def _gelu_tanh(x):
    return 0.5 * x * (1.0 + jnp.tanh(float(np.sqrt(2.0 / np.pi)) * (x + 0.044715 * (x * x * x))))


def _ffn_body(final, x_ref, hist_ref, g2_ref, wa_ref, wb_ref, cw_ref, wd_ref, gf_ref, *rest):
    if final:
        o_ref, new_ref, yn_ref, ext = rest
    else:
        o_ref, new_ref, ext = rest
    tm = x_ref.shape[1]

    @pl.when(pl.program_id(1) == 0)
    def _():
        ext[0:8, :] = hist_ref[0]

    x = x_ref[0]
    hn = _rmsnorm(x, g2_ref[...]).astype(BF16)
    ext[8:8 + tm, :] = jnp.dot(hn, wa_ref[...], preferred_element_type=F32)
    ub = jnp.dot(hn, wb_ref[...], preferred_element_type=F32)
    c2 = (cw_ref[0:1, :] * ext[6:6 + tm, :] + cw_ref[1:2, :] * ext[7:7 + tm, :]
          + cw_ref[2:3, :] * ext[8:8 + tm, :])
    out = x + _dot(_gelu_tanh(c2) * ub, wd_ref[...])
    o_ref[0] = out
    last = ext[tm:tm + 8, :]
    new_ref[0] = last
    ext[0:8, :] = last
    if final:
        yn_ref[0] = _rmsnorm(out, gf_ref[...])


def _ffn(x, hist, lw, norm_f_g, final, tm):
    B, T, _ = x.shape
    row = pl.BlockSpec((1, tm, D_MODEL), lambda b, i: (b, i, 0))
    hist_spec = pl.BlockSpec((1, 8, D_FF), lambda b, i: (b, 0, 0))
    consts = [lw['norm2_g'], lw['w_up_a'], lw['w_up_b'], lw['ffn_conv_w'], lw['w_down'], norm_f_g]
    out_specs = [row, hist_spec]
    out_shape = [jax.ShapeDtypeStruct((B, T, D_MODEL), F32), jax.ShapeDtypeStruct((B, 8, D_FF), F32)]
    if final:
        out_specs.append(row)
        out_shape.append(jax.ShapeDtypeStruct((B, T, D_MODEL), F32))
    return pl.pallas_call(
        functools.partial(_ffn_body, final), grid=(B, T // tm),
        in_specs=[row, hist_spec] + [_const_spec(c.shape) for c in consts],
        out_specs=out_specs, out_shape=out_shape,
        scratch_shapes=[pltpu.VMEM((8 + tm, D_FF), F32)],
        compiler_params=_params(("arbitrary", "arbitrary")),
        name="ffn",
    )(x, hist, *consts)


def _layer_weights(l, p):
    w_in = p['w_in'][l]
    o_f = 3 * W_HEADS
    o_glu = o_f + N_HEAD
    o_r = o_glu + 2 * W_HEADS
    o_g = o_r + C_SHIFT
    row = lambda a: a[l].reshape(1, -1)
    wwa = jnp.zeros((LANES, 2 * W_HEADS), F32)
    wwa = wwa.at[:64, :W_HEADS].set(p['rwkv_w_lora_up'][l]).at[64:, W_HEADS:].set(p['rwkv_a_lora_up'][l])
    head = np.arange(W_HEADS) // D_HEAD
    return {
        'norm1_g': row(p['norm1_g']),
        'w_qkv': w_in[:, :o_f].astype(BF16),
        'w_f': jnp.pad(w_in[:, o_f:o_glu], ((0, 0), (0, LANES - N_HEAD))).astype(BF16),
        'b_f': jnp.pad(p['b_f'][l], (0, LANES - N_HEAD)).reshape(1, LANES),
        'w_glu': w_in[:, o_glu:o_r].astype(BF16),
        'w_r': w_in[:, o_r:o_g].astype(BF16),
        'w_g': w_in[:, o_g:].astype(BF16),
        'b_gate': row(p['b_gate']),
        'conv_w': p['conv_w'][l], 'conv_b': row(p['conv_b']), 'conv_ln_g': row(p['conv_ln_g']),
        'conv_ln_b': row(p['conv_ln_b']),
        'rwkv_mu': row(p['rwkv_mu']), 'rwkv_w0': row(p['rwkv_w0']), 'rwkv_a0': row(p['rwkv_a0']),
        'rwkv_k_k': row(p['rwkv_k_k']), 'rwkv_k_a': row(p['rwkv_k_a']), 'rwkv_r_k': row(p['rwkv_r_k']),
        'w_wa': wwa.astype(BF16), 'rwkv_g_lora_up': p['rwkv_g_lora_up'][l].astype(BF16),
        'bd_ones': jnp.asarray(head[:, None] == head[None, :], BF16),
        'rwkv_gn_g': row(p['rwkv_gn_g']), 'rwkv_gn_b': row(p['rwkv_gn_b']),
        'w_oa': p['w_oa'][l].astype(BF16), 'w_ob': p['w_ob'][l].astype(BF16), 'w_oc': p['w_oc'][l].astype(BF16),
        'w_out': p['w_out'][l].astype(BF16),
        'norm2_g': row(p['norm2_g']),
        'w_up_a': p['w_up_ffn'][l][:, :D_FF].astype(BF16), 'w_up_b': p['w_up_ffn'][l][:, D_FF:].astype(BF16),
        'ffn_conv_w': p['ffn_conv_w'][l], 'w_down': p['w_down'][l].astype(BF16),
    }


def _pad_rows_front(a, rows):
    return jnp.pad(a, ((0, 0), (rows - a.shape[1], 0), (0, 0)))


def _state_to_pairs(s):
    B = s.shape[0]
    a = jnp.swapaxes(s, -1, -2).reshape(B, N_PAIR, 2, D_HEAD, D_HEAD)
    z = jnp.zeros_like(a[:, :, 0])
    top = jnp.concatenate([a[:, :, 0], z], axis=-1)
    bot = jnp.concatenate([z, a[:, :, 1]], axis=-1)
    return jnp.concatenate([top, bot], axis=-2)


def _pairs_to_state(a):
    B = a.shape[0]
    even = a[:, :, :D_HEAD, :D_HEAD]
    odd = a[:, :, D_HEAD:, D_HEAD:]
    s = jnp.stack([even, odd], axis=2).reshape(B, N_HEAD, D_HEAD, D_HEAD)
    return jnp.swapaxes(s, -1, -2)


def _tile(t, pref):
    return pref if t % pref == 0 else t


def _layer(x, lw, attend, conv_hist, shift_hist, wkv0, ffn_hist, norm_f_g, final, prompt):
    B, T, _ = x.shape
    if prompt:
        qc, kc, k, v, vb, logf, u, p, gates = _in_proj(x, lw, True, _tile(T, 256))
        o_a = attend(qc, kc, vb)
        seq = lambda a: a
        unseq = lambda a: a
    else:
        flat = x.reshape(1, B * T, D_MODEL)
        q, k, v, logf, u, p, gates = [a.reshape((B, T) + a.shape[2:]) for a in _in_proj(flat, lw, False, B * T)]
        o_a = attend(q, k, v, logf)
        seq = lambda a: a
        unseq = lambda a: a
    if T >= CONV_W - 1:
        conv_new = u[:, T - (CONV_W - 1):]
    else:
        conv_new = jnp.concatenate([conv_hist[:, T:], u], axis=1)
    o_b = _conv(u, _pad_rows_front(conv_hist, HIST_ROWS), lw, _tile(T, 256))
    shift_new = p[:, -1:]
    r, lwd, k_eff, vc, kk, beta, g, bonus = _rwkv_prep(p, _pad_rows_front(shift_hist, 8), lw, _tile(T, 256))
    chunk = 64
    if T % chunk:
        pad = lambda a: jnp.pad(a, ((0, 0), (0, chunk - T % chunk), (0, 0)))
        y, a_new = _rwkv_scan(pad(r), pad(lwd), pad(k_eff), pad(vc), pad(kk), pad(beta), _state_to_pairs(wkv0), chunk)
        y = y[:, :T]
    else:
        y, a_new = _rwkv_scan(r, lwd, k_eff, vc, kk, beta, _state_to_pairs(wkv0), chunk)
    wkv_new = _pairs_to_state(a_new)
    if prompt:
        x = _merge(x, o_a, o_b, y, bonus, g, gates, lw, _tile(T, 256))
    else:
        fl = lambda a: a.reshape((1, B * T) + a.shape[2:])
        x = _merge(fl(x), fl(o_a), fl(o_b), fl(y), fl(bonus), fl(g), fl(gates), lw, B * T).reshape(B, T, D_MODEL)
    outs = _ffn(x, _pad_rows_front(ffn_hist, 8), lw, norm_f_g, final, _tile(T, 256))
    x = outs[0]
    ffn_new = outs[1][:, 8 - (FFN_CONV_W - 1):]
    xn = outs[2] if final else None
    state = (k.reshape(B, T, N_HEAD, D_HEAD), v.reshape(B, T, N_HEAD, D_HEAD), logf, conv_new, shift_new,
             wkv_new, ffn_new)
    return x, xn, state


def kernel(x_prompt, x_sample, cache_k, cache_v, cache_logf, state_conv, state_shift, state_wkv, state_ffn,
           page_table, norm1_g, w_in, b_f, b_gate, w_oa, conv_w, conv_b, conv_ln_g, conv_ln_b, w_ob,
           rwkv_mu, rwkv_w0, rwkv_w_lora_up, rwkv_a0, rwkv_a_lora_up, rwkv_g_lora_up, rwkv_k_k, rwkv_k_a,
           rwkv_r_k, rwkv_gn_g, rwkv_gn_b, w_oc, w_out, norm2_g, w_up_ffn, ffn_conv_w, w_down, norm_f_g):
    p = dict(norm1_g=norm1_g, w_in=w_in, b_f=b_f, b_gate=b_gate, w_oa=w_oa, conv_w=conv_w, conv_b=conv_b,
             conv_ln_g=conv_ln_g, conv_ln_b=conv_ln_b, w_ob=w_ob, rwkv_mu=rwkv_mu, rwkv_w0=rwkv_w0,
             rwkv_w_lora_up=rwkv_w_lora_up, rwkv_a0=rwkv_a0, rwkv_a_lora_up=rwkv_a_lora_up,
             rwkv_g_lora_up=rwkv_g_lora_up, rwkv_k_k=rwkv_k_k, rwkv_k_a=rwkv_k_a, rwkv_r_k=rwkv_r_k,
             rwkv_gn_g=rwkv_gn_g, rwkv_gn_b=rwkv_gn_b, w_oc=w_oc, w_out=w_out, norm2_g=norm2_g,
             w_up_ffn=w_up_ffn, ffn_conv_w=ffn_conv_w, w_down=w_down)
    depth = w_in.shape[0]
    nf = norm_f_g.reshape(1, D_MODEL)
    weights = [_layer_weights(l, p) for l in range(depth)]

    bp, tp, _ = x_prompt.shape
    bs, ts, _ = x_sample.shape
    page = cache_k.shape[2]

    yp, p_states = x_prompt, []
    for l in range(depth):
        attend = lambda qc, kc, vb: _fox_prompt(qc, kc, vb, _tile(tp, 256))
        yp, ypn, st = _layer(yp, weights[l], attend,
                             jnp.zeros((bp, CONV_W - 1, W_HEADS), F32), jnp.zeros((bp, 1, C_SHIFT), F32),
                             jnp.zeros((bp, N_HEAD, D_HEAD, D_HEAD), F32), jnp.zeros((bp, FFN_CONV_W - 1, D_FF), F32),
                             nf, l == depth - 1, True)
        p_states.append(st)

    ys, s_states = x_sample, []
    n_pages = page_table.shape[1]
    n_group = 8 if n_pages % 8 == 0 else 1
    for l in range(depth):
        k_pool = cache_k[l].reshape(-1, page, W_HEADS)
        v_pool = cache_v[l].reshape(-1, page, W_HEADS)
        lf_pool_t = jnp.swapaxes(cache_logf[l], 1, 2)

        def attend(q, k, v, logf, k_pool=k_pool, v_pool=v_pool, lf_pool_t=lf_pool_t):
            padk = lambda a: jnp.pad(a, ((0, 0), (0, page - ts), (0, 0)))
            lf_t = jnp.pad(jnp.swapaxes(logf, 1, 2), ((0, 0), (0, 0), (0, page - ts)))
            return _fox_sample(q, padk(k), padk(v), lf_t, k_pool, v_pool, lf_pool_t, page_table, n_group)

        ys, ysn, st = _layer(ys, weights[l], attend, state_conv[l], state_shift[l], state_wkv[l], state_ffn[l],
                             nf, l == depth - 1, False)
        s_states.append(st)

    stk = lambda states, i: jnp.stack([s[i] for s in states], axis=0)
    return (ypn, ysn,
            stk(p_states, 0), stk(p_states, 1), stk(p_states, 2), stk(p_states, 3),
            stk(p_states, 4), stk(p_states, 5), stk(p_states, 6),
            stk(s_states, 0), stk(s_states, 1), stk(s_states, 2), stk(s_states, 3),
            stk(s_states, 4), stk(s_states, 5), stk(s_states, 6))
```

```python
import functools

import numpy as np
import jax
import jax.numpy as jnp
from jax import lax
from jax.experimental import pallas as pl
from jax.experimental.pallas import tpu as pltpu

F32 = jnp.float32
BF16 = jnp.bfloat16

D_MODEL = 1024
N_HEAD = 8
D_HEAD = 64
W_HEADS = N_HEAD * D_HEAD
LANES = 128
N_PAIR = W_HEADS // LANES
C_SHIFT = 3 * W_HEADS + 64 + 64 + 128
D_FF = 2816
CONV_W = 31
FFN_CONV_W = 3
RMS_EPS = 1e-6
LN_EPS = 1e-5
GN_EPS = 64e-5
L2_EPS = 1e-12
NEG_BIG = -1e30
LOG2E = float(np.log2(np.e))
VMEM_LIMIT = 56 * 2**20


def _params(sem):
    return pltpu.CompilerParams(dimension_semantics=sem, vmem_limit_bytes=VMEM_LIMIT)


def _const_spec(shape):
    nd = len(shape)
    return pl.BlockSpec(shape, lambda *_: (0,) * nd)


def _dot(a, b):
    return jnp.dot(a.astype(BF16), b.astype(BF16), preferred_element_type=F32)


def _dot_nt(a, b):
    return lax.dot_general(a.astype(BF16), b.astype(BF16), (((1,), (1,)), ((), ())),
                           preferred_element_type=F32)


def _split3(x):
    hi = x.astype(BF16).astype(F32)
    r1 = x - hi
    mid = r1.astype(BF16).astype(F32)
    lo = (r1 - mid).astype(BF16).astype(F32)
    return hi, mid, lo


def _tri_dot(tri, x):
    pieces = jnp.concatenate(_split3(x), axis=1).astype(BF16)
    r = jnp.dot(tri, pieces, preferred_element_type=F32)
    return r[:, :LANES] + r[:, LANES:2 * LANES] + r[:, 2 * LANES:]


def _dot_tri(x, tri):
    pieces = jnp.concatenate(_split3(x), axis=0).astype(BF16)
    r = jnp.dot(pieces, tri, preferred_element_type=F32)
    n = x.shape[0]
    return r[:n] + r[n:2 * n] + r[2 * n:]


def _log_sigmoid(z):
    return jnp.minimum(z, 0.0) - jnp.log1p(jnp.exp(-jnp.abs(z)))


def _rmsnorm(x, g):
    return x * lax.rsqrt(jnp.mean(x * x, axis=-1, keepdims=True) + RMS_EPS) * g


def _inproj_body(prompt, x_ref, g_ref, wqkv_ref, wf_ref, bf_ref, wglu_ref, wr_ref, wg_ref, bg_ref, *rest):
    if prompt:
        (eplace_ref, qc_ref, kc_ref, k_ref, v_ref, vb_ref, lf_ref, u_ref, p_ref, gate_ref, fcarry) = rest
    else:
        (q_ref, k_ref, v_ref, lf_ref, u_ref, p_ref, gate_ref) = rest
    tm = x_ref.shape[1]
    hn = _rmsnorm(x_ref[0], g_ref[...]).astype(BF16)

    qkv = jnp.dot(hn, wqkv_ref[...], preferred_element_type=F32)
    q = qkv[:, :W_HEADS] * (D_HEAD ** -0.5 * (LOG2E if prompt else 1.0))
    k = qkv[:, W_HEADS:2 * W_HEADS]
    v = qkv[:, 2 * W_HEADS:]
    k_ref[0] = k
    v_ref[0] = v

    logf = _log_sigmoid(jnp.dot(hn, wf_ref[...], preferred_element_type=F32) + bf_ref[...])
    lf_ref[0] = logf[:, :N_HEAD]

    glu = jnp.dot(hn, wglu_ref[...], preferred_element_type=F32)
    u_ref[0] = glu[:, :W_HEADS] * jax.nn.sigmoid(glu[:, W_HEADS:])
    p_ref[0] = jnp.dot(hn, wr_ref[...], preferred_element_type=F32)
    gate_ref[0] = jax.nn.sigmoid(jnp.dot(hn, wg_ref[...], preferred_element_type=F32)
                                 + bg_ref[...]).astype(gate_ref.dtype)

    if not prompt:
        q_ref[0] = q.astype(q_ref.dtype)
        return

    vb_ref[0] = v.astype(BF16)

    @pl.when(pl.program_id(1) == 0)
    def _():
        fcarry[...] = jnp.zeros_like(fcarry)

    lane = lax.broadcasted_iota(jnp.int32, (1, LANES), 1)
    lf = jnp.where(lane < N_HEAD, logf, 0.0)
    row = lax.broadcasted_iota(jnp.int32, (tm, tm), 0)
    col = lax.broadcasted_iota(jnp.int32, (tm, tm), 1)
    tri = jnp.where(col <= row, 1.0, 0.0).astype(BF16)
    fcum = _tri_dot(tri, lf) + fcarry[...]
    fcarry[...] = fcum[tm - 1:tm, :]
    aug = jnp.dot(jnp.concatenate(_split3(fcum * LOG2E), axis=1).astype(BF16), eplace_ref[...],
                  preferred_element_type=F32)
    slot = lane & 15
    augq = aug[:, :LANES] + jnp.where((slot >= 3) & (slot < 6), 1.0, 0.0)
    augk = aug[:, LANES:] + jnp.where(slot < 3, 1.0, 0.0)
    augk_b = augk.astype(BF16)
    for pair in range(N_PAIR):
        kc_ref[0, pair] = jnp.concatenate([k[:, pair * LANES:(pair + 1) * LANES].astype(BF16), augk_b], axis=1)
        q_pair = q[:, pair * LANES:(pair + 1) * LANES]
        for par in range(2):
            h = 2 * pair + par
            q_h = jnp.where((lane >= D_HEAD) if par else (lane < D_HEAD), q_pair, 0.0)
            a_h = jnp.where((lane >> 4) == h, augq, 0.0)
            qc_ref[0, h] = jnp.concatenate([q_h, a_h], axis=1).astype(BF16)


def _eplace():
    e = np.zeros((3 * LANES, 2 * LANES), np.float32)
    for piece in range(3):
        for h in range(N_HEAD):
            e[piece * LANES + h, 16 * h + piece] = 1.0
            e[piece * LANES + h, LANES + 16 * h + 3 + piece] = -1.0
    return jnp.asarray(e, BF16)


def _in_proj(x, lw, prompt, tm):
    B, T, _ = x.shape
    nt = T // tm
    row = lambda width: pl.BlockSpec((1, tm, width), lambda b, i: (b, i, 0))
    weights = [lw['norm1_g'], lw['w_qkv'], lw['w_f'], lw['b_f'], lw['w_glu'], lw['w_r'], lw['w_g'], lw['b_gate']]
    in_specs = [row(D_MODEL)] + [_const_spec(w.shape) for w in weights]
    common_shapes = [
        jax.ShapeDtypeStruct((B, T, W_HEADS), F32),
        jax.ShapeDtypeStruct((B, T, W_HEADS), F32),
    ]
    tail_shapes = [
        jax.ShapeDtypeStruct((B, T, N_HEAD), F32),
        jax.ShapeDtypeStruct((B, T, W_HEADS), F32),
        jax.ShapeDtypeStruct((B, T, C_SHIFT), F32),
        jax.ShapeDtypeStruct((B, T, 3 * D_MODEL), BF16),
    ]
    tail_specs = [row(N_HEAD), row(W_HEADS), row(C_SHIFT), row(3 * D_MODEL)]
    if prompt:
        args = [x] + weights + [_eplace()]
        in_specs = in_specs + [_const_spec((3 * LANES, 2 * LANES))]
        out_shape = [jax.ShapeDtypeStruct((B, N_HEAD, T, 2 * LANES), BF16),
                     jax.ShapeDtypeStruct((B, N_PAIR, T, 2 * LANES), BF16)] + common_shapes + \
                    [jax.ShapeDtypeStruct((B, T, W_HEADS), BF16)] + tail_shapes
        out_specs = [pl.BlockSpec((1, N_HEAD, tm, 2 * LANES), lambda b, i: (b, 0, i, 0)),
                     pl.BlockSpec((1, N_PAIR, tm, 2 * LANES), lambda b, i: (b, 0, i, 0)),
                     row(W_HEADS), row(W_HEADS), row(W_HEADS)] + tail_specs
        scratch = [pltpu.VMEM((1, LANES), F32)]
    else:
        args = [x] + weights
        out_shape = [jax.ShapeDtypeStruct((B, T, W_HEADS), BF16)] + common_shapes + tail_shapes
        out_specs = [row(W_HEADS), row(W_HEADS), row(W_HEADS)] + tail_specs
        scratch = []
    return pl.pallas_call(
        functools.partial(_inproj_body, prompt),
        grid=(B, nt), in_specs=in_specs, out_specs=out_specs, out_shape=out_shape,
        scratch_shapes=scratch, compiler_params=_params(("arbitrary", "arbitrary")),
        name="in_proj_prompt" if prompt else "in_proj_sample",
    )(*args)


def _fox_prompt_body(qc_ref, kc_ref, v_ref, o_ref, v1, *, tq, tk):
    i = pl.program_id(2)

    n_pp = kc_ref.shape[1]
    heads = range(2 * n_pp)

    @pl.when(i == 0)
    def _():
        for pp in range(n_pp):
            v1[pp, :, :LANES] = v_ref[0, :, pp * LANES:(pp + 1) * LANES]
            v1[pp, :, LANES:] = jnp.ones((v1.shape[1], LANES), BF16)

    qs = [qc_ref[0, h] for h in heads]
    def tile(start, width, carry, masked):
        kcs = [kc_ref[0, pp, pl.ds(start, width), :] for pp in range(n_pp)]
        vvs = [v1[pp, pl.ds(start, width), :] for pp in range(n_pp)]
        ss = [lax.dot_general(qs[h], kcs[h // 2], (((1,), (1,)), ((), ())), preferred_element_type=F32)
              for h in heads]
        if masked:
            row = i * tq + lax.broadcasted_iota(jnp.int32, (tq, width), 0)
            col = start + lax.broadcasted_iota(jnp.int32, (tq, width), 1)
            ss = [jnp.where(col <= row, s, NEG_BIG) for s in ss]
        m_new = [jnp.maximum(carry[h][0], jnp.max(ss[h], axis=-1, keepdims=True)) for h in heads]
        ps = [jnp.exp2(ss[h] - m_new[h]).astype(BF16) for h in heads]
        accs = [jnp.exp2(carry[h][0] - m_new[h]) * carry[h][1]
                + jnp.dot(ps[h], vvs[h // 2], preferred_element_type=F32) for h in heads]
        return tuple((m_new[h], accs[h]) for h in heads)

    init = tuple((jnp.full((tq, 1), NEG_BIG, F32), jnp.zeros((tq, 2 * LANES), F32)) for _ in heads)
    n_full = (i * tq) // tk
    carry = lax.fori_loop(0, n_full, lambda j, c: tile(pl.multiple_of(j * tk, tk), tk, c, False), init)
    final = tile(pl.multiple_of(n_full * tk, tk), tk, carry, True)
    lane = lax.broadcasted_iota(jnp.int32, (1, LANES), 1)
    for pp in range(n_pp):
        a0, a1 = final[2 * pp][1], final[2 * pp + 1][1]
        o_ref[0, :, pp * LANES:(pp + 1) * LANES] = jnp.where(
            lane < D_HEAD, a0[:, :LANES] / a0[:, LANES:], a1[:, :LANES] / a1[:, LANES:]).astype(o_ref.dtype)


def _fox_prompt(qc, kc, vb, tq, tk, n_pp=2):
    B, _, T, _ = qc.shape
    return pl.pallas_call(
        functools.partial(_fox_prompt_body, tq=tq, tk=tk),
        grid=(B, N_PAIR // n_pp, T // tq),
        in_specs=[pl.BlockSpec((1, 2 * n_pp, tq, 2 * LANES), lambda b, p, i: (b, p, i, 0)),
                  pl.BlockSpec((1, n_pp, T, 2 * LANES), lambda b, p, i: (b, p, 0, 0)),
                  pl.BlockSpec((1, T, n_pp * LANES), lambda b, p, i: (b, 0, p))],
        out_specs=pl.BlockSpec((1, tq, n_pp * LANES), lambda b, p, i: (b, i, p)),
        out_shape=jax.ShapeDtypeStruct((B, T, W_HEADS), BF16),
        scratch_shapes=[pltpu.VMEM((n_pp, T, 2 * LANES), BF16)],
        compiler_params=_params(("arbitrary", "arbitrary", "arbitrary")),
        name="fox_prompt",
    )(qc, kc, vb)


def _fox_sample_body(n_group, t_new, pt_ref, q_ref, kn_ref, vn_ref, lfn_ref, *rest):
    k_refs = rest[:n_group]
    v_refs = rest[n_group:2 * n_group]
    lf_refs = rest[2 * n_group:3 * n_group]
    o_ref, qbd, m_s, l_s, acc_s, tail_s, newsum_s = rest[3 * n_group:]
    j = pl.program_id(1)
    nrow = t_new * N_HEAD
    page = lf_refs[0].shape[1]
    lane_w = lax.broadcasted_iota(jnp.int32, (N_HEAD, W_HEADS), 1)
    head_w = lax.broadcasted_iota(jnp.int32, (N_HEAD, W_HEADS), 0)
    bd_mask = (lane_w >> 6) == head_w
    ri = lax.broadcasted_iota(jnp.int32, (page, page), 0)
    ci = lax.broadcasted_iota(jnp.int32, (page, page), 1)

    def tile4(x):
        return jnp.concatenate([x] * t_new, axis=0)

    def online(s, pv_fn):
        m_new = jnp.maximum(m_s[...], jnp.max(s, axis=-1, keepdims=True))
        alpha = jnp.exp(m_s[...] - m_new)
        p = jnp.exp(s - m_new)
        l_s[...] = alpha * l_s[...] + jnp.sum(p, axis=-1, keepdims=True)
        acc_s[...] = alpha * acc_s[...] + pv_fn(p.astype(BF16))
        m_s[...] = m_new

    @pl.when(j == 0)
    def _():
        q = q_ref[0].astype(F32)
        qbd[...] = jnp.concatenate(
            [jnp.where(bd_mask, jnp.broadcast_to(q[t:t + 1, :], (N_HEAD, W_HEADS)), 0.0)
             for t in range(t_new)], axis=0).astype(BF16)
        m_s[...] = jnp.full_like(m_s, NEG_BIG)
        l_s[...] = jnp.zeros_like(l_s)
        acc_s[...] = jnp.zeros_like(acc_s)
        tail_s[...] = jnp.zeros_like(tail_s)
        tri_incl = jnp.where(ri <= ci, 1.0, 0.0).astype(BF16)
        cum = _dot_tri(lfn_ref[0], tri_incl)
        newsum = jnp.concatenate([cum[:, t:t + 1] for t in range(t_new)], axis=0)
        newsum_s[...] = newsum
        s = _dot_nt(qbd[...], kn_ref[0])
        rr = lax.broadcasted_iota(jnp.int32, (nrow, page), 0) >> 3
        cc = lax.broadcasted_iota(jnp.int32, (nrow, page), 1)
        s = jnp.where(cc <= rr, s + newsum - tile4(cum), NEG_BIG)
        online(s, lambda p: _dot(p, vn_ref[0]))

    tri_after = jnp.where(ri > ci, 1.0, 0.0).astype(BF16)
    lfs = [lf_refs[g][...] for g in range(n_group)]
    pieces = jnp.concatenate([jnp.concatenate(_split3(lf), axis=0) for lf in lfs], axis=0).astype(BF16)
    suffix = jnp.dot(pieces, tri_after, preferred_element_type=F32)
    qb = qbd[...]
    tail = tail_s[...]
    parts = []
    for g in range(n_group):
        r0 = 3 * N_HEAD * g
        bias8 = (suffix[r0:r0 + N_HEAD] + suffix[r0 + N_HEAD:r0 + 2 * N_HEAD]
                 + suffix[r0 + 2 * N_HEAD:r0 + 3 * N_HEAD] + tail)
        parts.append(_dot(qb, k_refs[g][...]) + tile4(bias8))
        tail = tail + jnp.sum(lfs[g], axis=-1, keepdims=True)
    tail_s[...] = tail
    s_all = jnp.concatenate(parts, axis=1) + newsum_s[...]

    def pv_pages(p):
        acc = _dot_nt(p[:, :page], v_refs[0][...])
        for g in range(1, n_group):
            acc = acc + _dot_nt(p[:, g * page:(g + 1) * page], v_refs[g][...])
        return acc

    online(s_all, pv_pages)

    @pl.when(j == pl.num_programs(1) - 1)
    def _():
        o = acc_s[...] / l_s[...]
        for t in range(t_new):
            o_ref[0, t:t + 1, :] = jnp.sum(jnp.where(bd_mask, o[t * N_HEAD:(t + 1) * N_HEAD, :], 0.0), axis=0,
                                           keepdims=True).astype(o_ref.dtype)


def _fox_sample(q, k_new, v_new, lf_new_t, layer, kt_pool, vt_pool, lf_pool_t, page_table, n_group):
    B, t_new, _ = q.shape
    page = kt_pool.shape[3]
    n_pages = page_table.shape[1]
    steps = n_pages // n_group

    def paged(width_shape):
        specs = []
        for g in range(n_group):
            specs.append(pl.BlockSpec((None, None) + width_shape,
                                      lambda b, j, pt, g=g: (layer, pt[b, n_pages - 1 - (j * n_group + g)], 0, 0)))
        return specs

    per_b = lambda shape: pl.BlockSpec((1,) + shape, lambda b, j, pt: (b, 0, 0))
    nrow = t_new * N_HEAD
    grid_spec = pltpu.PrefetchScalarGridSpec(
        num_scalar_prefetch=1, grid=(B, steps),
        in_specs=[per_b((t_new, W_HEADS)), per_b((page, W_HEADS)), per_b((page, W_HEADS)), per_b((N_HEAD, page))]
        + paged((W_HEADS, page)) + paged((W_HEADS, page)) + paged((N_HEAD, page)),
        out_specs=per_b((t_new, W_HEADS)),
        scratch_shapes=[pltpu.VMEM((nrow, W_HEADS), BF16), pltpu.VMEM((nrow, 1), F32), pltpu.VMEM((nrow, 1), F32),
                        pltpu.VMEM((nrow, W_HEADS), F32), pltpu.VMEM((N_HEAD, 1), F32), pltpu.VMEM((nrow, 1), F32)])
    return pl.pallas_call(
        functools.partial(_fox_sample_body, n_group, t_new),
        grid_spec=grid_spec,
        out_shape=jax.ShapeDtypeStruct((B, t_new, W_HEADS), F32),
        compiler_params=_params(("arbitrary", "arbitrary")),
        name="fox_sample",
    )(page_table, q, k_new, v_new, lf_new_t, *([kt_pool] * n_group), *([vt_pool] * n_group),
      *([lf_pool_t] * n_group))


HIST_ROWS = 32


def _conv_body(u_ref, hist_ref, w_ref, b_ref, g_ref, beta_ref, o_ref, ext):
    tt = u_ref.shape[1]

    @pl.when(pl.program_id(1) == 0)
    def _():
        ext[0:HIST_ROWS, :] = hist_ref[0]

    ext[HIST_ROWS:HIST_ROWS + tt, :] = u_ref[0]
    first = HIST_ROWS - (CONV_W - 1)
    acc = jnp.zeros((tt, W_HEADS), F32)
    for tap in range(CONV_W):
        acc = acc + w_ref[tap:tap + 1, :] * ext[first + tap:first + tap + tt, :]
    c = acc + b_ref[...]
    mean = jnp.mean(c, axis=-1, keepdims=True)
    xc = c - mean
    var = jnp.mean(xc * xc, axis=-1, keepdims=True)
    y = xc * lax.rsqrt(var + LN_EPS) * g_ref[...] + beta_ref[...]
    o_ref[0] = (y * jax.nn.sigmoid(y)).astype(o_ref.dtype)
    ext[0:HIST_ROWS, :] = ext[tt:tt + HIST_ROWS, :]


def _conv(u, hist, lw, tt):
    B, T, _ = u.shape
    return pl.pallas_call(
        _conv_body, grid=(B, T // tt),
        in_specs=[pl.BlockSpec((1, tt, W_HEADS), lambda b, i: (b, i, 0)),
                  pl.BlockSpec((1, HIST_ROWS, W_HEADS), lambda b, i: (b, 0, 0)),
                  _const_spec((CONV_W, W_HEADS)), _const_spec((1, W_HEADS)), _const_spec((1, W_HEADS)),
                  _const_spec((1, W_HEADS))],
        out_specs=pl.BlockSpec((1, tt, W_HEADS), lambda b, i: (b, i, 0)),
        out_shape=jax.ShapeDtypeStruct((B, T, W_HEADS), BF16),
        scratch_shapes=[pltpu.VMEM((HIST_ROWS + tt, W_HEADS), F32)],
        compiler_params=_params(("arbitrary", "arbitrary")),
        name="conv",
    )(u, hist, lw['conv_w'], lw['conv_b'], lw['conv_ln_g'], lw['conv_ln_b'])


def _rwkv_prep_body(p_ref, hist_ref, mu_ref, w0_ref, a0_ref, kk_ref, ka_ref, rk_ref, wwa_ref, wg_ref, bd_ref,
                    r_o, lw_o, k_o, v_o, kk_o, b_o, g_o, bonus_o, ext):
    tt = p_ref.shape[1]

    @pl.when(pl.program_id(1) == 0)
    def _():
        ext[0:8, :] = hist_ref[0]

    p = p_ref[0]
    ext[8:8 + tt, :] = p
    prev = ext[7:7 + tt, :]
    ps = p + (prev - p) * mu_ref[...]
    ext[0:8, :] = ext[tt:tt + 8, :]

    r = ps[:, :W_HEADS]
    kc = ps[:, W_HEADS:2 * W_HEADS]
    vc = ps[:, 2 * W_HEADS:3 * W_HEADS]
    wa = ps[:, 3 * W_HEADS:3 * W_HEADS + LANES]
    gd = ps[:, 3 * W_HEADS + LANES:]
    lane = lax.broadcasted_iota(jnp.int32, (1, LANES), 1)
    wa = jnp.where(lane < 64, jnp.tanh(wa), wa)
    lora = _dot(wa, wwa_ref[...])
    lw_o[0] = -float(np.exp(-0.5)) * jax.nn.sigmoid(w0_ref[...] + lora[:, :W_HEADS])
    a = jax.nn.sigmoid(a0_ref[...] + lora[:, W_HEADS:])
    g_o[0] = _dot(jax.nn.sigmoid(gd), wg_ref[...])
    bd = bd_ref[...]
    kk = kc * kk_ref[...]
    sq = kk * kk
    sq_hi = sq.astype(BF16)
    sq_lo = (sq - sq_hi.astype(F32)).astype(BF16)
    ss = jnp.dot(sq_hi, bd, preferred_element_type=F32) + jnp.dot(sq_lo, bd, preferred_element_type=F32)
    kk = kk / jnp.maximum(jnp.sqrt(ss), L2_EPS)
    k_eff = kc * (1.0 + (a - 1.0) * ka_ref[...])
    r_o[0] = r
    k_o[0] = k_eff
    v_o[0] = vc
    kk_o[0] = kk
    b_o[0] = kk * a
    bonus_o[0] = _dot(r * k_eff * rk_ref[...], bd) * vc


def _rwkv_prep(p, hist, lw, tt):
    B, T, _ = p.shape
    row = lambda width: pl.BlockSpec((1, tt, width), lambda b, i: (b, i, 0))
    consts = [lw['rwkv_mu'], lw['rwkv_w0'], lw['rwkv_a0'], lw['rwkv_k_k'], lw['rwkv_k_a'], lw['rwkv_r_k'],
              lw['w_wa'], lw['rwkv_g_lora_up'], lw['bd_ones']]
    return pl.pallas_call(
        _rwkv_prep_body, grid=(B, T // tt),
        in_specs=[row(C_SHIFT), pl.BlockSpec((1, 8, C_SHIFT), lambda b, i: (b, 0, 0))]
        + [_const_spec(c.shape) for c in consts],
        out_specs=[row(W_HEADS)] * 8,
        out_shape=[jax.ShapeDtypeStruct((B, T, W_HEADS), F32)] * 8,
        scratch_shapes=[pltpu.VMEM((8 + tt, C_SHIFT), F32)],
        compiler_params=_params(("arbitrary", "arbitrary")),
        name="rwkv_prep",
    )(p, hist, *consts)


def _rwkv_scan_body(r_ref, lw_ref, k_ref, v_ref, kk_ref, b_ref, a0_ref, y_ref, a_out_ref, a_s, *, chunk, nb):
    c = chunk
    n = 2 * c

    @pl.when(pl.program_id(1) == 0)
    def _():
        a_s[...] = a0_ref[...]

    ri = lax.broadcasted_iota(jnp.int32, (n, n), 0)
    ci = lax.broadcasted_iota(jnp.int32, (n, n), 1)
    strict = ri > ci
    incl = ri >= ci
    eye_f = jnp.where(ri == ci, 1.0, 0.0)
    tr = lax.broadcasted_iota(jnp.int32, (c, c), 0)
    tc = lax.broadcasted_iota(jnp.int32, (c, c), 1)
    tri_incl = jnp.where(tc <= tr, 1.0, 0.0).astype(BF16)
    lane = lax.broadcasted_iota(jnp.int32, (1, LANES), 1)
    m_even = lane < D_HEAD

    def stack(x):
        return jnp.concatenate([jnp.where(m_even, x, 0.0), jnp.where(m_even, 0.0, x)], axis=0)

    units = [(bi, pair) for bi in range(nb) for pair in range(N_PAIR)]

    def load(ref):
        return [ref[bi, :, pair * LANES:(pair + 1) * LANES] for bi, pair in units]

    lw = load(lw_ref)
    cum = [_tri_dot(tri_incl, x) for x in lw]
    cum_end = [x[c - 1:c, :] for x in cum]
    e_neg = [jnp.exp(-x) for x in cum]
    e_end = [jnp.exp(ce - x) for ce, x in zip(cum_end, cum)]
    r, k, v, kk, b = load(r_ref), load(k_ref), load(v_ref), load(kk_ref), load(b_ref)
    rt = [stack(x * jnp.exp(cu)).astype(BF16) for x, cu in zip(r, cum)]
    ka = [stack(x * jnp.exp(cu - l)).astype(BF16) for x, cu, l in zip(kk, cum, lw)]
    kt = [stack(x * e).astype(BF16) for x, e in zip(k, e_neg)]
    bt = [stack(x * e).astype(BF16) for x, e in zip(b, e_neg)]
    kh_t = [stack(x * e).T.astype(BF16) for x, e in zip(k, e_end)]
    bh_t = [stack(x * e).T.astype(BF16) for x, e in zip(b, e_end)]
    vs = [stack(x).astype(BF16) for x in v]

    big = [_dot_nt(jnp.concatenate([a, q], axis=0), jnp.concatenate([x, y], axis=0))
           for a, q, x, y in zip(ka, rt, kt, bt)]
    m_mat = [jnp.where(strict, x[:n, :n], 0.0).astype(BF16) for x in big]
    l_mat = [jnp.where(strict, x[:n, n:], 0.0) for x in big]
    n1 = [jnp.where(incl, x[n:, :n], 0.0).astype(BF16) for x in big]
    n2 = [jnp.where(incl, x[n:, n:], 0.0).astype(BF16) for x in big]

    inv = [eye_f - jnp.where((ri >> 1) == (ci >> 1), x, 0.0) for x in l_mat]
    size = 2
    shift = 1
    while size < c:
        blk = ((ri >> (shift + 1)) == (ci >> (shift + 1))) & (((ri >> shift) & 1) == 1) & (((ci >> shift) & 1) == 0)
        inv_b = [x.astype(BF16) for x in inv]
        half = [_dot(x, jnp.where(blk, l, 0.0)) for x, l in zip(inv_b, l_mat)]
        inv = [x - _dot(h, xb) for x, h, xb in zip(inv, half, inv_b)]
        size *= 2
        shift += 1
    inv_b = [x.astype(BF16) for x in inv]

    mv = [_dot(m, x).astype(BF16) for m, x in zip(m_mat, vs)]
    wu = [_dot(i, jnp.concatenate([a, x], axis=1)).astype(BF16) for i, a, x in zip(inv_b, ka, mv)]
    nw = [_dot(x, w) for x, w in zip(n2, wu)]
    rq = [q.astype(F32) - x[:, :LANES] for q, x in zip(rt, nw)]
    y0 = [_dot(m, x) - w[:, LANES:] for m, x, w in zip(n1, vs, nw)]
    bw = [_dot(x, w) for x, w in zip(bh_t, wu)]
    eye_l = lax.broadcasted_iota(jnp.int32, (LANES, LANES), 0) == lax.broadcasted_iota(jnp.int32, (LANES, LANES), 1)
    g_mat = [jnp.where(eye_l, jnp.broadcast_to(jnp.exp(ce), (LANES, LANES)), 0.0) - x[:, :LANES]
             for ce, x in zip(cum_end, bw)]
    h_mat = [_dot(x, y) - w[:, LANES:] for x, y, w in zip(kh_t, vs, bw)]
    a_prev = [a_s[bi, pair].astype(BF16) for bi, pair in units]
    ys = [_dot(q, a) + y for q, a, y in zip(rq, a_prev, y0)]
    a_new = [_dot(g, a) + h for g, a, h in zip(g_mat, a_prev, h_mat)]
    for (bi, pair), y, a in zip(units, ys, a_new):
        y_ref[bi, :, pair * LANES:(pair + 1) * LANES] = y[:c] + y[c:]
        a_s[bi, pair] = a

    @pl.when(pl.program_id(1) == pl.num_programs(1) - 1)
    def _():
        a_out_ref[...] = a_s[...]


def _rwkv_scan(r, lwd, k, v, kk, b, a0, chunk, nb):
    B, T, _ = r.shape
    row = pl.BlockSpec((nb, chunk, W_HEADS), lambda bb, i: (bb, i, 0))
    st = pl.BlockSpec((nb, N_PAIR, LANES, LANES), lambda bb, i: (bb, 0, 0, 0))
    return pl.pallas_call(
        functools.partial(_rwkv_scan_body, chunk=chunk, nb=nb), grid=(B // nb, T // chunk),
        in_specs=[row] * 6 + [st], out_specs=[row, st],
        out_shape=[jax.ShapeDtypeStruct((B, T, W_HEADS), F32),
                   jax.ShapeDtypeStruct((B, N_PAIR, LANES, LANES), F32)],
        scratch_shapes=[pltpu.VMEM((nb, N_PAIR, LANES, LANES), F32)],
        compiler_params=_params(("arbitrary", "arbitrary")),
        name="rwkv_scan",
    )(r, lwd, k, v, kk, b, a0)


def _merge_body(x_ref, oa_ref, ob_ref, y_ref, bonus_ref, g_ref, gate_ref, woa_ref, wob_ref, woc_ref, wout_ref,
                gng_ref, gnb_ref, bd_ref, o_ref):
    bd = bd_ref[...]
    y = y_ref[0]
    mean = _dot(y, bd) * (1.0 / D_HEAD)
    xc = y - mean
    var = _dot(xc * xc, bd) * (1.0 / D_HEAD)
    yn = xc * lax.rsqrt(var + GN_EPS) * gng_ref[...] + gnb_ref[...]
    oc = (yn + bonus_ref[0]) * g_ref[0]
    gates = gate_ref[0].astype(F32)
    merged = (gates[:, :D_MODEL] * _dot(oa_ref[0], woa_ref[...])
              + gates[:, D_MODEL:2 * D_MODEL] * _dot(ob_ref[0], wob_ref[...])
              + gates[:, 2 * D_MODEL:] * _dot(oc, woc_ref[...]))
    o_ref[0] = x_ref[0] + _dot(merged, wout_ref[...])


def _merge(x, oa, ob, y, bonus, g, gates, lw, tm):
    B, T, _ = x.shape
    row = lambda width: pl.BlockSpec((1, tm, width), lambda b, i: (b, i, 0))
    consts = [lw['w_oa'], lw['w_ob'], lw['w_oc'], lw['w_out'], lw['rwkv_gn_g'], lw['rwkv_gn_b'], lw['bd_ones']]
    return pl.pallas_call(
        _merge_body, grid=(B, T // tm),
        in_specs=[row(D_MODEL), row(W_HEADS), row(W_HEADS), row(W_HEADS), row(W_HEADS), row(W_HEADS),
                  row(3 * D_MODEL)] + [_const_spec(c.shape) for c in consts],
        out_specs=row(D_MODEL), out_shape=jax.ShapeDtypeStruct((B, T, D_MODEL), F32),
        compiler_params=_params(("arbitrary", "arbitrary")),
        name="merge",
    )(x, oa, ob, y, bonus, g, gates, *consts)


def _gelu_tanh(x):
    return 0.5 * x * (1.0 + jnp.tanh(float(np.sqrt(2.0 / np.pi)) * (x + 0.044715 * (x * x * x))))


def _ffn_body(final, x_ref, hist_ref, g2_ref, wa_ref, wb_ref, cw_ref, wd_ref, gf_ref, *rest):
    if final:
        o_ref, new_ref, yn_ref, ext = rest
    else:
        o_ref, new_ref, ext = rest
    tm = x_ref.shape[1]

    @pl.when(pl.program_id(1) == 0)
    def _():
        ext[0:8, :] = hist_ref[0]

    x = x_ref[0]
    hn = _rmsnorm(x, g2_ref[...]).astype(BF16)
    ext[8:8 + tm, :] = jnp.dot(hn, wa_ref[...], preferred_element_type=F32)
    ub = jnp.dot(hn, wb_ref[...], preferred_element_type=F32)
    c2 = (cw_ref[0:1, :] * ext[6:6 + tm, :] + cw_ref[1:2, :] * ext[7:7 + tm, :]
          + cw_ref[2:3, :] * ext[8:8 + tm, :])
    out = x + _dot(_gelu_tanh(c2) * ub, wd_ref[...])
    o_ref[0] = out
    last = ext[tm:tm + 8, :]
    new_ref[0] = last
    ext[0:8, :] = last
    if final:
        yn_ref[0] = _rmsnorm(out, gf_ref[...])


def _ffn(x, hist, lw, norm_f_g, final, tm):
    B, T, _ = x.shape
    row = pl.BlockSpec((1, tm, D_MODEL), lambda b, i: (b, i, 0))
    hist_spec = pl.BlockSpec((1, 8, D_FF), lambda b, i: (b, 0, 0))
    consts = [lw['norm2_g'], lw['w_up_a'], lw['w_up_b'], lw['ffn_conv_w'], lw['w_down'], norm_f_g]
    out_specs = [row, hist_spec]
    out_shape = [jax.ShapeDtypeStruct((B, T, D_MODEL), F32), jax.ShapeDtypeStruct((B, 8, D_FF), F32)]
    if final:
        out_specs.append(row)
        out_shape.append(jax.ShapeDtypeStruct((B, T, D_MODEL), F32))
    return pl.pallas_call(
        functools.partial(_ffn_body, final), grid=(B, T // tm),
        in_specs=[row, hist_spec] + [_const_spec(c.shape) for c in consts],
        out_specs=out_specs, out_shape=out_shape,
        scratch_shapes=[pltpu.VMEM((8 + tm, D_FF), F32)],
        compiler_params=_params(("arbitrary", "arbitrary")),
        name="ffn",
    )(x, hist, *consts)


def _layer_weights(l, p):
    w_in = p['w_in'][l]
    o_f = 3 * W_HEADS
    o_glu = o_f + N_HEAD
    o_r = o_glu + 2 * W_HEADS
    o_g = o_r + C_SHIFT
    row = lambda a: a[l].reshape(1, -1)
    wwa = jnp.zeros((LANES, 2 * W_HEADS), F32)
    wwa = wwa.at[:64, :W_HEADS].set(p['rwkv_w_lora_up'][l]).at[64:, W_HEADS:].set(p['rwkv_a_lora_up'][l])
    head = np.arange(W_HEADS) // D_HEAD
    return {
        'norm1_g': row(p['norm1_g']),
        'w_qkv': w_in[:, :o_f].astype(BF16),
        'w_f': jnp.pad(w_in[:, o_f:o_glu], ((0, 0), (0, LANES - N_HEAD))).astype(BF16),
        'b_f': jnp.pad(p['b_f'][l], (0, LANES - N_HEAD)).reshape(1, LANES),
        'w_glu': w_in[:, o_glu:o_r].astype(BF16),
        'w_r': w_in[:, o_r:o_g].astype(BF16),
        'w_g': w_in[:, o_g:].astype(BF16),
        'b_gate': row(p['b_gate']),
        'conv_w': p['conv_w'][l], 'conv_b': row(p['conv_b']), 'conv_ln_g': row(p['conv_ln_g']),
        'conv_ln_b': row(p['conv_ln_b']),
        'rwkv_mu': row(p['rwkv_mu']), 'rwkv_w0': row(p['rwkv_w0']), 'rwkv_a0': row(p['rwkv_a0']),
        'rwkv_k_k': row(p['rwkv_k_k']), 'rwkv_k_a': row(p['rwkv_k_a']), 'rwkv_r_k': row(p['rwkv_r_k']),
        'w_wa': wwa.astype(BF16), 'rwkv_g_lora_up': p['rwkv_g_lora_up'][l].astype(BF16),
        'bd_ones': jnp.asarray(head[:, None] == head[None, :], BF16),
        'rwkv_gn_g': row(p['rwkv_gn_g']), 'rwkv_gn_b': row(p['rwkv_gn_b']),
        'w_oa': p['w_oa'][l].astype(BF16), 'w_ob': p['w_ob'][l].astype(BF16), 'w_oc': p['w_oc'][l].astype(BF16),
        'w_out': p['w_out'][l].astype(BF16),
        'norm2_g': row(p['norm2_g']),
        'w_up_a': p['w_up_ffn'][l][:, :D_FF].astype(BF16), 'w_up_b': p['w_up_ffn'][l][:, D_FF:].astype(BF16),
        'ffn_conv_w': p['ffn_conv_w'][l], 'w_down': p['w_down'][l].astype(BF16),
    }


def _pad_rows_front(a, rows):
    return jnp.pad(a, ((0, 0), (rows - a.shape[1], 0), (0, 0)))


def _state_to_pairs(s):
    B = s.shape[0]
    a = jnp.swapaxes(s, -1, -2).reshape(B, N_PAIR, 2, D_HEAD, D_HEAD)
    z = jnp.zeros_like(a[:, :, 0])
    top = jnp.concatenate([a[:, :, 0], z], axis=-1)
    bot = jnp.concatenate([z, a[:, :, 1]], axis=-1)
    return jnp.concatenate([top, bot], axis=-2)


def _pairs_to_state(a):
    B = a.shape[0]
    even = a[:, :, :D_HEAD, :D_HEAD]
    odd = a[:, :, D_HEAD:, D_HEAD:]
    s = jnp.stack([even, odd], axis=2).reshape(B, N_HEAD, D_HEAD, D_HEAD)
    return jnp.swapaxes(s, -1, -2)


def _tile(t, pref):
    return pref if t % pref == 0 else t


def _layer(x, lw, attend, conv_hist, shift_hist, wkv0, ffn_hist, norm_f_g, final, prompt):
    B, T, _ = x.shape
    if prompt:
        qc, kc, k, v, vb, logf, u, p, gates = _in_proj(x, lw, True, _tile(T, 256))
        o_a = attend(qc, kc, vb)
        seq = lambda a: a
        unseq = lambda a: a
    else:
        flat = x.reshape(1, B * T, D_MODEL)
        q, k, v, logf, u, p, gates = [a.reshape((B, T) + a.shape[2:]) for a in _in_proj(flat, lw, False, B * T)]
        o_a = attend(q, k, v, logf)
        seq = lambda a: a
        unseq = lambda a: a
    if T >= CONV_W - 1:
        conv_new = u[:, T - (CONV_W - 1):]
    else:
        conv_new = jnp.concatenate([conv_hist[:, T:], u], axis=1)
    o_b = _conv(u, _pad_rows_front(conv_hist, HIST_ROWS), lw, _tile(T, 256))
    shift_new = p[:, -1:]
    r, lwd, k_eff, vc, kk, beta, g, bonus = _rwkv_prep(p, _pad_rows_front(shift_hist, 8), lw, _tile(T, 256))
    chunk = 64
    nb = 2 if B % 2 == 0 else 1
    if T % chunk:
        pad = lambda a: jnp.pad(a, ((0, 0), (0, chunk - T % chunk), (0, 0)))
        y, a_new = _rwkv_scan(pad(r), pad(lwd), pad(k_eff), pad(vc), pad(kk), pad(beta), _state_to_pairs(wkv0),
                              chunk, nb)
        y = y[:, :T]
    else:
        y, a_new = _rwkv_scan(r, lwd, k_eff, vc, kk, beta, _state_to_pairs(wkv0), chunk, nb)
    wkv_new = _pairs_to_state(a_new)
    if prompt:
        x = _merge(x, o_a, o_b, y, bonus, g, gates, lw, _tile(T, 256))
    else:
        fl = lambda a: a.reshape((1, B * T) + a.shape[2:])
        x = _merge(fl(x), fl(o_a), fl(o_b), fl(y), fl(bonus), fl(g), fl(gates), lw, B * T).reshape(B, T, D_MODEL)
    outs = _ffn(x, _pad_rows_front(ffn_hist, 8), lw, norm_f_g, final, _tile(T, 256))
    x = outs[0]
    ffn_new = outs[1][:, 8 - (FFN_CONV_W - 1):]
    xn = outs[2] if final else None
    state = (k.reshape(B, T, N_HEAD, D_HEAD), v.reshape(B, T, N_HEAD, D_HEAD), logf, conv_new, shift_new,
             wkv_new, ffn_new)
    return x, xn, state


def kernel(x_prompt, x_sample, cache_k, cache_v, cache_logf, state_conv, state_shift, state_wkv, state_ffn,
           page_table, norm1_g, w_in, b_f, b_gate, w_oa, conv_w, conv_b, conv_ln_g, conv_ln_b, w_ob,
           rwkv_mu, rwkv_w0, rwkv_w_lora_up, rwkv_a0, rwkv_a_lora_up, rwkv_g_lora_up, rwkv_k_k, rwkv_k_a,
           rwkv_r_k, rwkv_gn_g, rwkv_gn_b, w_oc, w_out, norm2_g, w_up_ffn, ffn_conv_w, w_down, norm_f_g):
    p = dict(norm1_g=norm1_g, w_in=w_in, b_f=b_f, b_gate=b_gate, w_oa=w_oa, conv_w=conv_w, conv_b=conv_b,
             conv_ln_g=conv_ln_g, conv_ln_b=conv_ln_b, w_ob=w_ob, rwkv_mu=rwkv_mu, rwkv_w0=rwkv_w0,
             rwkv_w_lora_up=rwkv_w_lora_up, rwkv_a0=rwkv_a0, rwkv_a_lora_up=rwkv_a_lora_up,
             rwkv_g_lora_up=rwkv_g_lora_up, rwkv_k_k=rwkv_k_k, rwkv_k_a=rwkv_k_a, rwkv_r_k=rwkv_r_k,
             rwkv_gn_g=rwkv_gn_g, rwkv_gn_b=rwkv_gn_b, w_oc=w_oc, w_out=w_out, norm2_g=norm2_g,
             w_up_ffn=w_up_ffn, ffn_conv_w=ffn_conv_w, w_down=w_down)
    depth = w_in.shape[0]
    nf = norm_f_g.reshape(1, D_MODEL)
    weights = [_layer_weights(l, p) for l in range(depth)]

    bp, tp, _ = x_prompt.shape
    bs, ts, _ = x_sample.shape
    page = cache_k.shape[2]

    yp, p_states = x_prompt, []
    for l in range(depth):
        attend = lambda qc, kc, vb: _fox_prompt(qc, kc, vb, _tile(tp, 256), _tile(tp, 1024))
        yp, ypn, st = _layer(yp, weights[l], attend,
                             jnp.zeros((bp, CONV_W - 1, W_HEADS), F32), jnp.zeros((bp, 1, C_SHIFT), F32),
                             jnp.zeros((bp, N_HEAD, D_HEAD, D_HEAD), F32), jnp.zeros((bp, FFN_CONV_W - 1, D_FF), F32),
                             nf, l == depth - 1, True)
        p_states.append(st)

    ys, s_states = x_sample, []
    n_pages = page_table.shape[1]
    n_group = 16 if n_pages % 16 == 0 else 1
    n_pool = cache_k.shape[1]
    kt_pool = jnp.transpose(cache_k, (0, 1, 3, 4, 2)).reshape(depth, n_pool, W_HEADS, page)
    vt_pool = jnp.transpose(cache_v, (0, 1, 3, 4, 2)).reshape(depth, n_pool, W_HEADS, page)
    lf_pool_t = jnp.swapaxes(cache_logf, 2, 3)
    for l in range(depth):

        def attend(q, k, v, logf, l=l):
            padk = lambda a: jnp.pad(a, ((0, 0), (0, page - ts), (0, 0)))
            lf_t = jnp.pad(jnp.swapaxes(logf, 1, 2), ((0, 0), (0, 0), (0, page - ts)))
            return _fox_sample(q, padk(k), padk(v), lf_t, l, kt_pool, vt_pool, lf_pool_t, page_table, n_group)

        ys, ysn, st = _layer(ys, weights[l], attend, state_conv[l], state_shift[l], state_wkv[l], state_ffn[l],
                             nf, l == depth - 1, False)
        s_states.append(st)

    stk = lambda states, i: jnp.stack([s[i] for s in states], axis=0)
    return (ypn, ysn,
            stk(p_states, 0), stk(p_states, 1), stk(p_states, 2), stk(p_states, 3),
            stk(p_states, 4), stk(p_states, 5), stk(p_states, 6),
            stk(s_states, 0), stk(s_states, 1), stk(s_states, 2), stk(s_states, 3),
            stk(s_states, 4), stk(s_states, 5), stk(s_states, 6))
```

```python
import functools

import numpy as np
import jax
import jax.numpy as jnp
from jax import lax
from jax.experimental import pallas as pl
from jax.experimental.pallas import tpu as pltpu

F32 = jnp.float32
BF16 = jnp.bfloat16

D_MODEL = 1024
N_HEAD = 8
D_HEAD = 64
W_HEADS = N_HEAD * D_HEAD
LANES = 128
N_PAIR = W_HEADS // LANES
C_SHIFT = 3 * W_HEADS + 64 + 64 + 128
D_FF = 2816
CONV_W = 31
FFN_CONV_W = 3
RMS_EPS = 1e-6
LN_EPS = 1e-5
GN_EPS = 64e-5
L2_EPS = 1e-12
NEG_BIG = -1e30
LOG2E = float(np.log2(np.e))
VMEM_LIMIT = 56 * 2**20


def _params(sem):
    return pltpu.CompilerParams(dimension_semantics=sem, vmem_limit_bytes=VMEM_LIMIT)


def _const_spec(shape):
    nd = len(shape)
    return pl.BlockSpec(shape, lambda *_: (0,) * nd)


def _dot(a, b):
    return jnp.dot(a.astype(BF16), b.astype(BF16), preferred_element_type=F32)


def _dot_nt(a, b):
    return lax.dot_general(a.astype(BF16), b.astype(BF16), (((1,), (1,)), ((), ())),
                           preferred_element_type=F32)


def _split3(x):
    hi = x.astype(BF16).astype(F32)
    r1 = x - hi
    mid = r1.astype(BF16).astype(F32)
    lo = (r1 - mid).astype(BF16).astype(F32)
    return hi, mid, lo


def _tri_dot(tri, x):
    pieces = jnp.concatenate(_split3(x), axis=1).astype(BF16)
    r = jnp.dot(tri, pieces, preferred_element_type=F32)
    return r[:, :LANES] + r[:, LANES:2 * LANES] + r[:, 2 * LANES:]


def _dot_tri(x, tri):
    pieces = jnp.concatenate(_split3(x), axis=0).astype(BF16)
    r = jnp.dot(pieces, tri, preferred_element_type=F32)
    n = x.shape[0]
    return r[:n] + r[n:2 * n] + r[2 * n:]


def _log_sigmoid(z):
    return jnp.minimum(z, 0.0) - jnp.log1p(jnp.exp(-jnp.abs(z)))


def _rmsnorm(x, g):
    return x * lax.rsqrt(jnp.mean(x * x, axis=-1, keepdims=True) + RMS_EPS) * g


def _inproj_body(prompt, x_ref, g_ref, wqkv_ref, wf_ref, bf_ref, wglu_ref, wr_ref, wg_ref, bg_ref, *rest):
    if prompt:
        (eplace_ref, qc_ref, kc_ref, k_ref, v_ref, vb_ref, lf_ref, u_ref, p_ref, gate_ref, fcarry) = rest
    else:
        (q_ref, k_ref, v_ref, lf_ref, u_ref, p_ref, gate_ref) = rest
    tm = x_ref.shape[1]
    hn = _rmsnorm(x_ref[0], g_ref[...]).astype(BF16)

    qkv = jnp.dot(hn, wqkv_ref[...], preferred_element_type=F32)
    q = qkv[:, :W_HEADS] * (D_HEAD ** -0.5 * (LOG2E if prompt else 1.0))
    k = qkv[:, W_HEADS:2 * W_HEADS]
    v = qkv[:, 2 * W_HEADS:]
    k_ref[0] = k
    v_ref[0] = v

    logf = _log_sigmoid(jnp.dot(hn, wf_ref[...], preferred_element_type=F32) + bf_ref[...])
    lf_ref[0] = logf[:, :N_HEAD]

    glu = jnp.dot(hn, wglu_ref[...], preferred_element_type=F32)
    u_ref[0] = glu[:, :W_HEADS] * jax.nn.sigmoid(glu[:, W_HEADS:])
    p_ref[0] = jnp.dot(hn, wr_ref[...], preferred_element_type=F32)
    gate_ref[0] = jax.nn.sigmoid(jnp.dot(hn, wg_ref[...], preferred_element_type=F32)
                                 + bg_ref[...]).astype(gate_ref.dtype)

    if not prompt:
        q_ref[0] = q.astype(q_ref.dtype)
        return

    v_t = v.T.astype(BF16)
    for pair in range(N_PAIR):
        vb_ref[0, pair] = v_t[pair * LANES:(pair + 1) * LANES, :]

    @pl.when(pl.program_id(1) == 0)
    def _():
        fcarry[...] = jnp.zeros_like(fcarry)

    lane = lax.broadcasted_iota(jnp.int32, (1, LANES), 1)
    lf = jnp.where(lane < N_HEAD, logf, 0.0)
    row = lax.broadcasted_iota(jnp.int32, (tm, tm), 0)
    col = lax.broadcasted_iota(jnp.int32, (tm, tm), 1)
    tri = jnp.where(col <= row, 1.0, 0.0).astype(BF16)
    fcum = _tri_dot(tri, lf) + fcarry[...]
    fcarry[...] = fcum[tm - 1:tm, :]
    aug = jnp.dot(jnp.concatenate(_split3(fcum * LOG2E), axis=1).astype(BF16), eplace_ref[...],
                  preferred_element_type=F32)
    slot = lane & 15
    augq = aug[:, :LANES] + jnp.where((slot >= 3) & (slot < 6), 1.0, 0.0)
    augk = aug[:, LANES:] + jnp.where(slot < 3, 1.0, 0.0)
    augk_b = augk.astype(BF16)
    for pair in range(N_PAIR):
        kc_ref[0, pair] = jnp.concatenate([k[:, pair * LANES:(pair + 1) * LANES].astype(BF16), augk_b], axis=1)
        q_pair = q[:, pair * LANES:(pair + 1) * LANES]
        for par in range(2):
            h = 2 * pair + par
            q_h = jnp.where((lane >= D_HEAD) if par else (lane < D_HEAD), q_pair, 0.0)
            a_h = jnp.where((lane >> 4) == h, augq, 0.0)
            qc_ref[0, h] = jnp.concatenate([q_h, a_h], axis=1).astype(BF16)


def _eplace():
    e = np.zeros((3 * LANES, 2 * LANES), np.float32)
    for piece in range(3):
        for h in range(N_HEAD):
            e[piece * LANES + h, 16 * h + piece] = 1.0
            e[piece * LANES + h, LANES + 16 * h + 3 + piece] = -1.0
    return jnp.asarray(e, BF16)


def _in_proj(x, lw, prompt, tm):
    B, T, _ = x.shape
    nt = T // tm
    row = lambda width: pl.BlockSpec((1, tm, width), lambda b, i: (b, i, 0))
    weights = [lw['norm1_g'], lw['w_qkv'], lw['w_f'], lw['b_f'], lw['w_glu'], lw['w_r'], lw['w_g'], lw['b_gate']]
    in_specs = [row(D_MODEL)] + [_const_spec(w.shape) for w in weights]
    common_shapes = [
        jax.ShapeDtypeStruct((B, T, W_HEADS), F32),
        jax.ShapeDtypeStruct((B, T, W_HEADS), F32),
    ]
    tail_shapes = [
        jax.ShapeDtypeStruct((B, T, N_HEAD), F32),
        jax.ShapeDtypeStruct((B, T, W_HEADS), F32),
        jax.ShapeDtypeStruct((B, T, C_SHIFT), F32),
        jax.ShapeDtypeStruct((B, T, 3 * D_MODEL), BF16),
    ]
    tail_specs = [row(N_HEAD), row(W_HEADS), row(C_SHIFT), row(3 * D_MODEL)]
    if prompt:
        args = [x] + weights + [_eplace()]
        in_specs = in_specs + [_const_spec((3 * LANES, 2 * LANES))]
        out_shape = [jax.ShapeDtypeStruct((B, N_HEAD, T, 2 * LANES), BF16),
                     jax.ShapeDtypeStruct((B, N_PAIR, T, 2 * LANES), BF16)] + common_shapes + \
                    [jax.ShapeDtypeStruct((B, N_PAIR, LANES, T), BF16)] + tail_shapes
        out_specs = [pl.BlockSpec((1, N_HEAD, tm, 2 * LANES), lambda b, i: (b, 0, i, 0)),
                     pl.BlockSpec((1, N_PAIR, tm, 2 * LANES), lambda b, i: (b, 0, i, 0)),
                     row(W_HEADS), row(W_HEADS),
                     pl.BlockSpec((1, N_PAIR, LANES, tm), lambda b, i: (b, 0, 0, i))] + tail_specs
        scratch = [pltpu.VMEM((1, LANES), F32)]
    else:
        args = [x] + weights
        out_shape = [jax.ShapeDtypeStruct((B, T, W_HEADS), BF16)] + common_shapes + tail_shapes
        out_specs = [row(W_HEADS), row(W_HEADS), row(W_HEADS)] + tail_specs
        scratch = []
    return pl.pallas_call(
        functools.partial(_inproj_body, prompt),
        grid=(B, nt), in_specs=in_specs, out_specs=out_specs, out_shape=out_shape,
        scratch_shapes=scratch, compiler_params=_params(("arbitrary", "arbitrary")),
        name="in_proj_prompt" if prompt else "in_proj_sample",
    )(*args)


ONES_ROWS = 16


def _fox_prompt_body(qc_ref, kc_ref, vt_ref, o_ref, v1t, *, tq, tk):
    i = pl.program_id(2)
    n_pp = kc_ref.shape[1]
    heads = range(2 * n_pp)
    n_kt = v1t.shape[1]

    @pl.when(i == 0)
    def _():
        for pp in range(n_pp):
            for j in range(n_kt):
                v1t[pp, j, :LANES, :] = vt_ref[0, pp, :, j * tk:(j + 1) * tk]
                v1t[pp, j, LANES:, :] = jnp.ones((ONES_ROWS, tk), BF16)

    qs = [qc_ref[0, h] for h in heads]

    def tile(j, carry, masked):
        start = pl.multiple_of(j * tk, tk)
        kcs = [kc_ref[0, pp, pl.ds(start, tk), :] for pp in range(n_pp)]
        vts = [v1t[pp, j] for pp in range(n_pp)]
        ss = [lax.dot_general(kcs[h // 2], qs[h], (((1,), (1,)), ((), ())), preferred_element_type=F32)
              for h in heads]
        if masked:
            key = start + lax.broadcasted_iota(jnp.int32, (tk, tq), 0)
            qpos = i * tq + lax.broadcasted_iota(jnp.int32, (tk, tq), 1)
            ss = [jnp.where(key <= qpos, s, NEG_BIG) for s in ss]
        m_new = [jnp.maximum(carry[h][0], jnp.max(ss[h], axis=0, keepdims=True)) for h in heads]
        ps = [jnp.exp2(ss[h] - m_new[h]).astype(BF16) for h in heads]
        accs = [jnp.exp2(carry[h][0] - m_new[h]) * carry[h][1]
                + jnp.dot(vts[h // 2], ps[h], preferred_element_type=F32) for h in heads]
        return tuple((m_new[h], accs[h]) for h in heads)

    init = tuple((jnp.full((1, tq), NEG_BIG, F32), jnp.zeros((LANES + ONES_ROWS, tq), F32)) for _ in heads)
    n_full = (i * tq) // tk
    carry = lax.fori_loop(0, n_full, lambda j, c: tile(j, c, False), init)
    final = tile(n_full, carry, True)
    chan = lax.broadcasted_iota(jnp.int32, (LANES, 1), 0)
    for pp in range(n_pp):
        a0, a1 = final[2 * pp][1], final[2 * pp + 1][1]
        o_t = jnp.where(chan < D_HEAD, a0[:LANES] / a0[LANES:LANES + 1], a1[:LANES] / a1[LANES:LANES + 1])
        o_ref[0, :, pp * LANES:(pp + 1) * LANES] = o_t.T.astype(o_ref.dtype)


def _fox_prompt(qc, kc, vt, tq, tk, n_pp=2):
    B, _, T, _ = qc.shape
    return pl.pallas_call(
        functools.partial(_fox_prompt_body, tq=tq, tk=tk),
        grid=(B, N_PAIR // n_pp, T // tq),
        in_specs=[pl.BlockSpec((1, 2 * n_pp, tq, 2 * LANES), lambda b, p, i: (b, p, i, 0)),
                  pl.BlockSpec((1, n_pp, T, 2 * LANES), lambda b, p, i: (b, p, 0, 0)),
                  pl.BlockSpec((1, n_pp, LANES, T), lambda b, p, i: (b, p, 0, 0))],
        out_specs=pl.BlockSpec((1, tq, n_pp * LANES), lambda b, p, i: (b, i, p)),
        out_shape=jax.ShapeDtypeStruct((B, T, W_HEADS), BF16),
        scratch_shapes=[pltpu.VMEM((n_pp, T // tk, LANES + ONES_ROWS, tk), BF16)],
        compiler_params=_params(("arbitrary", "arbitrary", "arbitrary")),
        name="fox_prompt",
    )(qc, kc, vt)


def _fox_sample_body(n_group, t_new, pt_ref, q_ref, kn_ref, vn_ref, lfn_ref, *rest):
    k_refs = rest[:n_group]
    v_refs = rest[n_group:2 * n_group]
    lf_refs = rest[2 * n_group:3 * n_group]
    o_ref, qbd, m_s, l_s, acc_s, tail_s, newsum_s = rest[3 * n_group:]
    j = pl.program_id(1)
    nrow = t_new * N_HEAD
    page = lf_refs[0].shape[1]
    lane_w = lax.broadcasted_iota(jnp.int32, (N_HEAD, W_HEADS), 1)
    head_w = lax.broadcasted_iota(jnp.int32, (N_HEAD, W_HEADS), 0)
    bd_mask = (lane_w >> 6) == head_w
    ri = lax.broadcasted_iota(jnp.int32, (page, page), 0)
    ci = lax.broadcasted_iota(jnp.int32, (page, page), 1)

    def tile4(x):
        return jnp.concatenate([x] * t_new, axis=0)

    def online(s, pv_fn):
        m_new = jnp.maximum(m_s[...], jnp.max(s, axis=-1, keepdims=True))
        alpha = jnp.exp(m_s[...] - m_new)
        p = jnp.exp(s - m_new)
        l_s[...] = alpha * l_s[...] + jnp.sum(p, axis=-1, keepdims=True)
        acc_s[...] = alpha * acc_s[...] + pv_fn(p.astype(BF16))
        m_s[...] = m_new

    @pl.when(j == 0)
    def _():
        q = q_ref[0].astype(F32)
        qbd[...] = jnp.concatenate(
            [jnp.where(bd_mask, jnp.broadcast_to(q[t:t + 1, :], (N_HEAD, W_HEADS)), 0.0)
             for t in range(t_new)], axis=0).astype(BF16)
        m_s[...] = jnp.full_like(m_s, NEG_BIG)
        l_s[...] = jnp.zeros_like(l_s)
        acc_s[...] = jnp.zeros_like(acc_s)
        tail_s[...] = jnp.zeros_like(tail_s)
        tri_incl = jnp.where(ri <= ci, 1.0, 0.0).astype(BF16)
        cum = _dot_tri(lfn_ref[0], tri_incl)
        newsum = jnp.concatenate([cum[:, t:t + 1] for t in range(t_new)], axis=0)
        newsum_s[...] = newsum
        s = _dot_nt(qbd[...], kn_ref[0])
        rr = lax.broadcasted_iota(jnp.int32, (nrow, page), 0) >> 3
        cc = lax.broadcasted_iota(jnp.int32, (nrow, page), 1)
        s = jnp.where(cc <= rr, s + newsum - tile4(cum), NEG_BIG)
        online(s, lambda p: _dot(p, vn_ref[0]))

    tri_after = jnp.where(ri > ci, 1.0, 0.0).astype(BF16)
    lfs = [lf_refs[g][...] for g in range(n_group)]
    pieces = jnp.concatenate([jnp.concatenate(_split3(lf), axis=0) for lf in lfs], axis=0).astype(BF16)
    suffix = jnp.dot(pieces, tri_after, preferred_element_type=F32)
    qb = qbd[...]
    tail = tail_s[...]
    parts = []
    for g in range(n_group):
        r0 = 3 * N_HEAD * g
        bias8 = (suffix[r0:r0 + N_HEAD] + suffix[r0 + N_HEAD:r0 + 2 * N_HEAD]
                 + suffix[r0 + 2 * N_HEAD:r0 + 3 * N_HEAD] + tail)
        parts.append(_dot(qb, k_refs[g][...]) + tile4(bias8))
        tail = tail + jnp.sum(lfs[g], axis=-1, keepdims=True)
    tail_s[...] = tail
    s_all = jnp.concatenate(parts, axis=1) + newsum_s[...]

    def pv_pages(p):
        acc = _dot_nt(p[:, :page], v_refs[0][...])
        for g in range(1, n_group):
            acc = acc + _dot_nt(p[:, g * page:(g + 1) * page], v_refs[g][...])
        return acc

    online(s_all, pv_pages)

    @pl.when(j == pl.num_programs(1) - 1)
    def _():
        o = acc_s[...] / l_s[...]
        for t in range(t_new):
            o_ref[0, t:t + 1, :] = jnp.sum(jnp.where(bd_mask, o[t * N_HEAD:(t + 1) * N_HEAD, :], 0.0), axis=0,
                                           keepdims=True).astype(o_ref.dtype)


def _fox_sample(q, k_new, v_new, lf_new_t, layer, kt_pool, vt_pool, lf_pool_t, page_table, n_group):
    B, t_new, _ = q.shape
    page = kt_pool.shape[3]
    n_pages = page_table.shape[1]
    steps = n_pages // n_group

    def paged(width_shape):
        specs = []
        for g in range(n_group):
            specs.append(pl.BlockSpec((None, None) + width_shape,
                                      lambda b, j, pt, g=g: (layer, pt[b, n_pages - 1 - (j * n_group + g)], 0, 0)))
        return specs

    per_b = lambda shape: pl.BlockSpec((1,) + shape, lambda b, j, pt: (b, 0, 0))
    nrow = t_new * N_HEAD
    grid_spec = pltpu.PrefetchScalarGridSpec(
        num_scalar_prefetch=1, grid=(B, steps),
        in_specs=[per_b((t_new, W_HEADS)), per_b((page, W_HEADS)), per_b((page, W_HEADS)), per_b((N_HEAD, page))]
        + paged((W_HEADS, page)) + paged((W_HEADS, page)) + paged((N_HEAD, page)),
        out_specs=per_b((t_new, W_HEADS)),
        scratch_shapes=[pltpu.VMEM((nrow, W_HEADS), BF16), pltpu.VMEM((nrow, 1), F32), pltpu.VMEM((nrow, 1), F32),
                        pltpu.VMEM((nrow, W_HEADS), F32), pltpu.VMEM((N_HEAD, 1), F32), pltpu.VMEM((nrow, 1), F32)])
    return pl.pallas_call(
        functools.partial(_fox_sample_body, n_group, t_new),
        grid_spec=grid_spec,
        out_shape=jax.ShapeDtypeStruct((B, t_new, W_HEADS), F32),
        compiler_params=_params(("arbitrary", "arbitrary")),
        name="fox_sample",
    )(page_table, q, k_new, v_new, lf_new_t, *([kt_pool] * n_group), *([vt_pool] * n_group),
      *([lf_pool_t] * n_group))


HIST_ROWS = 32


SUBLANES = 8
CONV_ROWS = 32


def _conv_body(u_ref, hist_ref, w_ref, b_ref, g_ref, beta_ref, o_ref, ext, shifted):
    tt = u_ref.shape[1]

    @pl.when(pl.program_id(1) == 0)
    def _():
        ext[0:HIST_ROWS, :] = hist_ref[0]

    ext[HIST_ROWS:HIST_ROWS + tt, :] = u_ref[0]
    span = tt + HIST_ROWS - SUBLANES
    for s in range(1, SUBLANES):
        shifted[s - 1, 0:span, :] = ext[s:s + span, :]
    first = HIST_ROWS - (CONV_W - 1)
    rows = min(tt, CONV_ROWS)
    for c0 in range(0, tt, rows):
        acc = jnp.zeros((rows, W_HEADS), F32)
        for tap in range(CONV_W):
            s = (first + tap) % SUBLANES
            base = first + tap - s + c0
            win = ext[base:base + rows, :] if s == 0 else shifted[s - 1, base:base + rows, :]
            acc = acc + w_ref[tap:tap + 1, :] * win
        c = acc + b_ref[...]
        mean = jnp.mean(c, axis=-1, keepdims=True)
        xc = c - mean
        var = jnp.mean(xc * xc, axis=-1, keepdims=True)
        y = xc * lax.rsqrt(var + LN_EPS) * g_ref[...] + beta_ref[...]
        o_ref[0, c0:c0 + rows, :] = (y * jax.nn.sigmoid(y)).astype(o_ref.dtype)
    ext[0:HIST_ROWS, :] = ext[tt:tt + HIST_ROWS, :]


def _conv(u, hist, lw, tt):
    B, T, _ = u.shape
    return pl.pallas_call(
        _conv_body, grid=(B, T // tt),
        in_specs=[pl.BlockSpec((1, tt, W_HEADS), lambda b, i: (b, i, 0)),
                  pl.BlockSpec((1, HIST_ROWS, W_HEADS), lambda b, i: (b, 0, 0)),
                  _const_spec((CONV_W, W_HEADS)), _const_spec((1, W_HEADS)), _const_spec((1, W_HEADS)),
                  _const_spec((1, W_HEADS))],
        out_specs=pl.BlockSpec((1, tt, W_HEADS), lambda b, i: (b, i, 0)),
        out_shape=jax.ShapeDtypeStruct((B, T, W_HEADS), BF16),
        scratch_shapes=[pltpu.VMEM((HIST_ROWS + tt, W_HEADS), F32),
                        pltpu.VMEM((SUBLANES - 1, HIST_ROWS + tt - SUBLANES, W_HEADS), F32)],
        compiler_params=_params(("arbitrary", "arbitrary")),
        name="conv",
    )(u, hist, lw['conv_w'], lw['conv_b'], lw['conv_ln_g'], lw['conv_ln_b'])


def _rwkv_prep_body(p_ref, hist_ref, mu_ref, w0_ref, a0_ref, kk_ref, ka_ref, rk_ref, wwa_ref, wg_ref, bd_ref,
                    r_o, lw_o, k_o, v_o, kk_o, b_o, g_o, bonus_o, ext):
    tt = p_ref.shape[1]

    @pl.when(pl.program_id(1) == 0)
    def _():
        ext[0:8, :] = hist_ref[0]

    p = p_ref[0]
    ext[8:8 + tt, :] = p
    prev = ext[7:7 + tt, :]
    ps = p + (prev - p) * mu_ref[...]
    ext[0:8, :] = ext[tt:tt + 8, :]

    r = ps[:, :W_HEADS]
    kc = ps[:, W_HEADS:2 * W_HEADS]
    vc = ps[:, 2 * W_HEADS:3 * W_HEADS]
    wa = ps[:, 3 * W_HEADS:3 * W_HEADS + LANES]
    gd = ps[:, 3 * W_HEADS + LANES:]
    lane = lax.broadcasted_iota(jnp.int32, (1, LANES), 1)
    wa = jnp.where(lane < 64, jnp.tanh(wa), wa)
    lora = _dot(wa, wwa_ref[...])
    lw_o[0] = -float(np.exp(-0.5)) * jax.nn.sigmoid(w0_ref[...] + lora[:, :W_HEADS])
    a = jax.nn.sigmoid(a0_ref[...] + lora[:, W_HEADS:])
    g_o[0] = _dot(jax.nn.sigmoid(gd), wg_ref[...])
    bd = bd_ref[...]
    kk = kc * kk_ref[...]
    sq = kk * kk
    sq_hi = sq.astype(BF16)
    sq_lo = (sq - sq_hi.astype(F32)).astype(BF16)
    ss = jnp.dot(sq_hi, bd, preferred_element_type=F32) + jnp.dot(sq_lo, bd, preferred_element_type=F32)
    kk = kk / jnp.maximum(jnp.sqrt(ss), L2_EPS)
    k_eff = kc * (1.0 + (a - 1.0) * ka_ref[...])
    r_o[0] = r
    k_o[0] = k_eff
    v_o[0] = vc
    kk_o[0] = kk
    b_o[0] = kk * a
    bonus_o[0] = _dot(r * k_eff * rk_ref[...], bd) * vc


def _rwkv_prep(p, hist, lw, tt):
    B, T, _ = p.shape
    row = lambda width: pl.BlockSpec((1, tt, width), lambda b, i: (b, i, 0))
    consts = [lw['rwkv_mu'], lw['rwkv_w0'], lw['rwkv_a0'], lw['rwkv_k_k'], lw['rwkv_k_a'], lw['rwkv_r_k'],
              lw['w_wa'], lw['rwkv_g_lora_up'], lw['bd_ones']]
    return pl.pallas_call(
        _rwkv_prep_body, grid=(B, T // tt),
        in_specs=[row(C_SHIFT), pl.BlockSpec((1, 8, C_SHIFT), lambda b, i: (b, 0, 0))]
        + [_const_spec(c.shape) for c in consts],
        out_specs=[row(W_HEADS)] * 8,
        out_shape=[jax.ShapeDtypeStruct((B, T, W_HEADS), F32)] * 8,
        scratch_shapes=[pltpu.VMEM((8 + tt, C_SHIFT), F32)],
        compiler_params=_params(("arbitrary", "arbitrary")),
        name="rwkv_prep",
    )(p, hist, *consts)


def _rwkv_scan_body(r_ref, lw_ref, k_ref, v_ref, kk_ref, b_ref, a0_ref, y_ref, a_out_ref, a_s, *, chunk, nb):
    c = chunk
    n = 2 * c

    @pl.when(pl.program_id(1) == 0)
    def _():
        a_s[...] = a0_ref[...]

    ri = lax.broadcasted_iota(jnp.int32, (n, n), 0)
    ci = lax.broadcasted_iota(jnp.int32, (n, n), 1)
    strict = ri > ci
    incl = ri >= ci
    eye_f = jnp.where(ri == ci, 1.0, 0.0)
    tr = lax.broadcasted_iota(jnp.int32, (c, c), 0)
    tc = lax.broadcasted_iota(jnp.int32, (c, c), 1)
    tri_incl = jnp.where(tc <= tr, 1.0, 0.0).astype(BF16)
    lane = lax.broadcasted_iota(jnp.int32, (1, LANES), 1)
    m_even = lane < D_HEAD

    def stack(x):
        return jnp.concatenate([jnp.where(m_even, x, 0.0), jnp.where(m_even, 0.0, x)], axis=0)

    units = [(bi, pair) for bi in range(nb) for pair in range(N_PAIR)]

    def load(ref):
        return [ref[bi, :, pair * LANES:(pair + 1) * LANES] for bi, pair in units]

    lw = load(lw_ref)
    cum = [_tri_dot(tri_incl, x) for x in lw]
    cum_end = [x[c - 1:c, :] for x in cum]
    e_neg = [jnp.exp(-x) for x in cum]
    e_end = [jnp.exp(ce - x) for ce, x in zip(cum_end, cum)]
    r, k, v, kk, b = load(r_ref), load(k_ref), load(v_ref), load(kk_ref), load(b_ref)
    rt = [stack(x * jnp.exp(cu)).astype(BF16) for x, cu in zip(r, cum)]
    ka = [stack(x * jnp.exp(cu - l)).astype(BF16) for x, cu, l in zip(kk, cum, lw)]
    kt = [stack(x * e).astype(BF16) for x, e in zip(k, e_neg)]
    bt = [stack(x * e).astype(BF16) for x, e in zip(b, e_neg)]
    kh_t = [stack(x * e).T.astype(BF16) for x, e in zip(k, e_end)]
    bh_t = [stack(x * e).T.astype(BF16) for x, e in zip(b, e_end)]
    vs = [stack(x).astype(BF16) for x in v]

    big = [_dot_nt(jnp.concatenate([a, q], axis=0), jnp.concatenate([x, y], axis=0))
           for a, q, x, y in zip(ka, rt, kt, bt)]
    m_mat = [jnp.where(strict, x[:n, :n], 0.0).astype(BF16) for x in big]
    l_mat = [jnp.where(strict, x[:n, n:], 0.0) for x in big]
    n1 = [jnp.where(incl, x[n:, :n], 0.0).astype(BF16) for x in big]
    n2 = [jnp.where(incl, x[n:, n:], 0.0).astype(BF16) for x in big]

    inv = [eye_f - jnp.where((ri >> 1) == (ci >> 1), x, 0.0) for x in l_mat]
    size = 2
    shift = 1
    while size < c:
        blk = ((ri >> (shift + 1)) == (ci >> (shift + 1))) & (((ri >> shift) & 1) == 1) & (((ci >> shift) & 1) == 0)
        inv_b = [x.astype(BF16) for x in inv]
        half = [_dot(x, jnp.where(blk, l, 0.0)) for x, l in zip(inv_b, l_mat)]
        inv = [x - _dot(h, xb) for x, h, xb in zip(inv, half, inv_b)]
        size *= 2
        shift += 1
    inv_b = [x.astype(BF16) for x in inv]

    mv = [_dot(m, x).astype(BF16) for m, x in zip(m_mat, vs)]
    wu = [_dot(i, jnp.concatenate([a, x], axis=1)).astype(BF16) for i, a, x in zip(inv_b, ka, mv)]
    nw = [_dot(x, w) for x, w in zip(n2, wu)]
    rq = [q.astype(F32) - x[:, :LANES] for q, x in zip(rt, nw)]
    y0 = [_dot(m, x) - w[:, LANES:] for m, x, w in zip(n1, vs, nw)]
    bw = [_dot(x, w) for x, w in zip(bh_t, wu)]
    eye_l = lax.broadcasted_iota(jnp.int32, (LANES, LANES), 0) == lax.broadcasted_iota(jnp.int32, (LANES, LANES), 1)
    g_mat = [jnp.where(eye_l, jnp.broadcast_to(jnp.exp(ce), (LANES, LANES)), 0.0) - x[:, :LANES]
             for ce, x in zip(cum_end, bw)]
    h_mat = [_dot(x, y) - w[:, LANES:] for x, y, w in zip(kh_t, vs, bw)]
    a_prev = [a_s[bi, pair].astype(BF16) for bi, pair in units]
    ys = [_dot(q, a) + y for q, a, y in zip(rq, a_prev, y0)]
    a_new = [_dot(g, a) + h for g, a, h in zip(g_mat, a_prev, h_mat)]
    for (bi, pair), y, a in zip(units, ys, a_new):
        y_ref[bi, :, pair * LANES:(pair + 1) * LANES] = y[:c] + y[c:]
        a_s[bi, pair] = a

    @pl.when(pl.program_id(1) == pl.num_programs(1) - 1)
    def _():
        a_out_ref[...] = a_s[...]


def _rwkv_scan(r, lwd, k, v, kk, b, a0, chunk, nb):
    B, T, _ = r.shape
    row = pl.BlockSpec((nb, chunk, W_HEADS), lambda bb, i: (bb, i, 0))
    st = pl.BlockSpec((nb, N_PAIR, LANES, LANES), lambda bb, i: (bb, 0, 0, 0))
    return pl.pallas_call(
        functools.partial(_rwkv_scan_body, chunk=chunk, nb=nb), grid=(B // nb, T // chunk),
        in_specs=[row] * 6 + [st], out_specs=[row, st],
        out_shape=[jax.ShapeDtypeStruct((B, T, W_HEADS), F32),
                   jax.ShapeDtypeStruct((B, N_PAIR, LANES, LANES), F32)],
        scratch_shapes=[pltpu.VMEM((nb, N_PAIR, LANES, LANES), F32)],
        compiler_params=_params(("arbitrary", "arbitrary")),
        name="rwkv_scan",
    )(r, lwd, k, v, kk, b, a0)


def _merge_body(x_ref, oa_ref, ob_ref, y_ref, bonus_ref, g_ref, gate_ref, woa_ref, wob_ref, woc_ref, wout_ref,
                gng_ref, gnb_ref, bd_ref, o_ref):
    bd = bd_ref[...]
    y = y_ref[0]
    mean = _dot(y, bd) * (1.0 / D_HEAD)
    xc = y - mean
    var = _dot(xc * xc, bd) * (1.0 / D_HEAD)
    yn = xc * lax.rsqrt(var + GN_EPS) * gng_ref[...] + gnb_ref[...]
    oc = (yn + bonus_ref[0]) * g_ref[0]
    gates = gate_ref[0].astype(F32)
    merged = (gates[:, :D_MODEL] * _dot(oa_ref[0], woa_ref[...])
              + gates[:, D_MODEL:2 * D_MODEL] * _dot(ob_ref[0], wob_ref[...])
              + gates[:, 2 * D_MODEL:] * _dot(oc, woc_ref[...]))
    o_ref[0] = x_ref[0] + _dot(merged, wout_ref[...])


def _merge(x, oa, ob, y, bonus, g, gates, lw, tm):
    B, T, _ = x.shape
    row = lambda width: pl.BlockSpec((1, tm, width), lambda b, i: (b, i, 0))
    consts = [lw['w_oa'], lw['w_ob'], lw['w_oc'], lw['w_out'], lw['rwkv_gn_g'], lw['rwkv_gn_b'], lw['bd_ones']]
    return pl.pallas_call(
        _merge_body, grid=(B, T // tm),
        in_specs=[row(D_MODEL), row(W_HEADS), row(W_HEADS), row(W_HEADS), row(W_HEADS), row(W_HEADS),
                  row(3 * D_MODEL)] + [_const_spec(c.shape) for c in consts],
        out_specs=row(D_MODEL), out_shape=jax.ShapeDtypeStruct((B, T, D_MODEL), F32),
        compiler_params=_params(("arbitrary", "arbitrary")),
        name="merge",
    )(x, oa, ob, y, bonus, g, gates, *consts)


def _gelu_tanh(x):
    return 0.5 * x * (1.0 + jnp.tanh(float(np.sqrt(2.0 / np.pi)) * (x + 0.044715 * (x * x * x))))


def _ffn_body(final, x_ref, hist_ref, g2_ref, wa_ref, wb_ref, cw_ref, wd_ref, gf_ref, *rest):
    if final:
        o_ref, new_ref, yn_ref, ext = rest
    else:
        o_ref, new_ref, ext = rest
    tm = x_ref.shape[1]

    @pl.when(pl.program_id(1) == 0)
    def _():
        ext[0:8, :] = hist_ref[0]

    x = x_ref[0]
    hn = _rmsnorm(x, g2_ref[...]).astype(BF16)
    ext[8:8 + tm, :] = jnp.dot(hn, wa_ref[...], preferred_element_type=F32)
    ub = jnp.dot(hn, wb_ref[...], preferred_element_type=F32)
    c2 = (cw_ref[0:1, :] * ext[6:6 + tm, :] + cw_ref[1:2, :] * ext[7:7 + tm, :]
          + cw_ref[2:3, :] * ext[8:8 + tm, :])
    out = x + _dot(_gelu_tanh(c2) * ub, wd_ref[...])
    o_ref[0] = out
    last = ext[tm:tm + 8, :]
    new_ref[0] = last
    ext[0:8, :] = last
    if final:
        yn_ref[0] = _rmsnorm(out, gf_ref[...])


def _ffn_short_body(final, t_seq, x_ref, prev_ref, g2_ref, wa_ref, wb_ref, cw_ref, wd_ref, gf_ref, *rest):
    if final:
        o_ref, ua_ref, yn_ref, ext = rest
    else:
        o_ref, ua_ref, ext = rest
    tm = x_ref.shape[1]
    x = x_ref[0]
    hn = _rmsnorm(x, g2_ref[...]).astype(BF16)
    ua = jnp.dot(hn, wa_ref[...], preferred_element_type=F32)
    ub = jnp.dot(hn, wb_ref[...], preferred_element_type=F32)
    ua_ref[0] = ua
    ext[0:8, :] = jnp.zeros((8, D_FF), F32)
    ext[8:8 + tm, :] = ua
    t = lax.rem(lax.broadcasted_iota(jnp.int32, (tm, 1), 0), t_seq)
    back1 = jnp.where(t >= 1, ext[7:7 + tm, :], prev_ref[0])
    back2 = jnp.where(t >= 2, ext[6:6 + tm, :], prev_ref[1])
    c2 = cw_ref[0:1, :] * back2 + cw_ref[1:2, :] * back1 + cw_ref[2:3, :] * ua
    out = x + _dot(_gelu_tanh(c2) * ub, wd_ref[...])
    o_ref[0] = out
    if final:
        yn_ref[0] = _rmsnorm(out, gf_ref[...])


def _ffn_short(x, hist, lw, norm_f_g, final):
    B, T, _ = x.shape
    M = B * T
    zeros = jnp.zeros((B, T, D_FF), F32)
    prev1 = zeros.at[:, 0].set(hist[:, 1])
    prev2 = zeros.at[:, 0].set(hist[:, 0]).at[:, 1].set(hist[:, 1])
    prev = jnp.stack([prev1.reshape(M, D_FF), prev2.reshape(M, D_FF)])
    row = _const_spec((1, M, D_MODEL))
    wide = _const_spec((1, M, D_FF))
    consts = [lw['norm2_g'], lw['w_up_a'], lw['w_up_b'], lw['ffn_conv_w'], lw['w_down'], norm_f_g]
    out_specs = [row, wide]
    out_shape = [jax.ShapeDtypeStruct((1, M, D_MODEL), F32), jax.ShapeDtypeStruct((1, M, D_FF), F32)]
    if final:
        out_specs.append(row)
        out_shape.append(jax.ShapeDtypeStruct((1, M, D_MODEL), F32))
    outs = pl.pallas_call(
        functools.partial(_ffn_short_body, final, T), grid=(1,),
        in_specs=[row, _const_spec((2, M, D_FF))] + [_const_spec(c.shape) for c in consts],
        out_specs=out_specs, out_shape=out_shape,
        scratch_shapes=[pltpu.VMEM((8 + M, D_FF), F32)],
        compiler_params=_params(("arbitrary",)),
        name="ffn_short",
    )(x.reshape(1, M, D_MODEL), prev, *consts)
    res = [outs[0].reshape(B, T, D_MODEL), outs[1].reshape(B, T, D_FF)[:, T - (FFN_CONV_W - 1):]]
    if final:
        res.append(outs[2].reshape(B, T, D_MODEL))
    return res


def _ffn(x, hist, lw, norm_f_g, final, tm):
    B, T, _ = x.shape
    row = pl.BlockSpec((1, tm, D_MODEL), lambda b, i: (b, i, 0))
    hist_spec = pl.BlockSpec((1, 8, D_FF), lambda b, i: (b, 0, 0))
    consts = [lw['norm2_g'], lw['w_up_a'], lw['w_up_b'], lw['ffn_conv_w'], lw['w_down'], norm_f_g]
    out_specs = [row, hist_spec]
    out_shape = [jax.ShapeDtypeStruct((B, T, D_MODEL), F32), jax.ShapeDtypeStruct((B, 8, D_FF), F32)]
    if final:
        out_specs.append(row)
        out_shape.append(jax.ShapeDtypeStruct((B, T, D_MODEL), F32))
    return pl.pallas_call(
        functools.partial(_ffn_body, final), grid=(B, T // tm),
        in_specs=[row, hist_spec] + [_const_spec(c.shape) for c in consts],
        out_specs=out_specs, out_shape=out_shape,
        scratch_shapes=[pltpu.VMEM((8 + tm, D_FF), F32)],
        compiler_params=_params(("arbitrary", "arbitrary")),
        name="ffn",
    )(x, hist, *consts)


def _layer_weights(l, p):
    w_in = p['w_in'][l]
    o_f = 3 * W_HEADS
    o_glu = o_f + N_HEAD
    o_r = o_glu + 2 * W_HEADS
    o_g = o_r + C_SHIFT
    row = lambda a: a[l].reshape(1, -1)
    wwa = jnp.zeros((LANES, 2 * W_HEADS), F32)
    wwa = wwa.at[:64, :W_HEADS].set(p['rwkv_w_lora_up'][l]).at[64:, W_HEADS:].set(p['rwkv_a_lora_up'][l])
    head = np.arange(W_HEADS) // D_HEAD
    return {
        'norm1_g': row(p['norm1_g']),
        'w_qkv': w_in[:, :o_f].astype(BF16),
        'w_f': jnp.pad(w_in[:, o_f:o_glu], ((0, 0), (0, LANES - N_HEAD))).astype(BF16),
        'b_f': jnp.pad(p['b_f'][l], (0, LANES - N_HEAD)).reshape(1, LANES),
        'w_glu': w_in[:, o_glu:o_r].astype(BF16),
        'w_r': w_in[:, o_r:o_g].astype(BF16),
        'w_g': w_in[:, o_g:].astype(BF16),
        'b_gate': row(p['b_gate']),
        'conv_w': p['conv_w'][l], 'conv_b': row(p['conv_b']), 'conv_ln_g': row(p['conv_ln_g']),
        'conv_ln_b': row(p['conv_ln_b']),
        'rwkv_mu': row(p['rwkv_mu']), 'rwkv_w0': row(p['rwkv_w0']), 'rwkv_a0': row(p['rwkv_a0']),
        'rwkv_k_k': row(p['rwkv_k_k']), 'rwkv_k_a': row(p['rwkv_k_a']), 'rwkv_r_k': row(p['rwkv_r_k']),
        'w_wa': wwa.astype(BF16), 'rwkv_g_lora_up': p['rwkv_g_lora_up'][l].astype(BF16),
        'bd_ones': jnp.asarray(head[:, None] == head[None, :], BF16),
        'rwkv_gn_g': row(p['rwkv_gn_g']), 'rwkv_gn_b': row(p['rwkv_gn_b']),
        'w_oa': p['w_oa'][l].astype(BF16), 'w_ob': p['w_ob'][l].astype(BF16), 'w_oc': p['w_oc'][l].astype(BF16),
        'w_out': p['w_out'][l].astype(BF16),
        'norm2_g': row(p['norm2_g']),
        'w_up_a': p['w_up_ffn'][l][:, :D_FF].astype(BF16), 'w_up_b': p['w_up_ffn'][l][:, D_FF:].astype(BF16),
        'ffn_conv_w': p['ffn_conv_w'][l], 'w_down': p['w_down'][l].astype(BF16),
    }


def _pad_rows_front(a, rows):
    return jnp.pad(a, ((0, 0), (rows - a.shape[1], 0), (0, 0)))


def _state_to_pairs(s):
    B = s.shape[0]
    a = jnp.swapaxes(s, -1, -2).reshape(B, N_PAIR, 2, D_HEAD, D_HEAD)
    z = jnp.zeros_like(a[:, :, 0])
    top = jnp.concatenate([a[:, :, 0], z], axis=-1)
    bot = jnp.concatenate([z, a[:, :, 1]], axis=-1)
    return jnp.concatenate([top, bot], axis=-2)


def _pairs_to_state(a):
    B = a.shape[0]
    even = a[:, :, :D_HEAD, :D_HEAD]
    odd = a[:, :, D_HEAD:, D_HEAD:]
    s = jnp.stack([even, odd], axis=2).reshape(B, N_HEAD, D_HEAD, D_HEAD)
    return jnp.swapaxes(s, -1, -2)


def _tile(t, pref):
    return pref if t % pref == 0 else t


def _layer(x, lw, attend, conv_hist, shift_hist, wkv0, ffn_hist, norm_f_g, final, prompt):
    B, T, _ = x.shape
    if prompt:
        qc, kc, k, v, vb, logf, u, p, gates = _in_proj(x, lw, True, _tile(T, 256))
        o_a = attend(qc, kc, vb)
        seq = lambda a: a
        unseq = lambda a: a
    else:
        flat = x.reshape(1, B * T, D_MODEL)
        q, k, v, logf, u, p, gates = [a.reshape((B, T) + a.shape[2:]) for a in _in_proj(flat, lw, False, B * T)]
        o_a = attend(q, k, v, logf)
        seq = lambda a: a
        unseq = lambda a: a
    if T >= CONV_W - 1:
        conv_new = u[:, T - (CONV_W - 1):]
    else:
        conv_new = jnp.concatenate([conv_hist[:, T:], u], axis=1)
    o_b = _conv(u, _pad_rows_front(conv_hist, HIST_ROWS), lw, _tile(T, 256))
    shift_new = p[:, -1:]
    r, lwd, k_eff, vc, kk, beta, g, bonus = _rwkv_prep(p, _pad_rows_front(shift_hist, 8), lw, _tile(T, 256))
    chunk = 64
    nb = 2 if B % 2 == 0 else 1
    if T % chunk:
        pad = lambda a: jnp.pad(a, ((0, 0), (0, chunk - T % chunk), (0, 0)))
        y, a_new = _rwkv_scan(pad(r), pad(lwd), pad(k_eff), pad(vc), pad(kk), pad(beta), _state_to_pairs(wkv0),
                              chunk, nb)
        y = y[:, :T]
    else:
        y, a_new = _rwkv_scan(r, lwd, k_eff, vc, kk, beta, _state_to_pairs(wkv0), chunk, nb)
    wkv_new = _pairs_to_state(a_new)
    if prompt:
        x = _merge(x, o_a, o_b, y, bonus, g, gates, lw, _tile(T, 256))
    else:
        fl = lambda a: a.reshape((1, B * T) + a.shape[2:])
        x = _merge(fl(x), fl(o_a), fl(o_b), fl(y), fl(bonus), fl(g), fl(gates), lw, B * T).reshape(B, T, D_MODEL)
    if prompt or T < FFN_CONV_W - 1:
        outs = _ffn(x, _pad_rows_front(ffn_hist, 8), lw, norm_f_g, final, _tile(T, 256))
        ffn_new = outs[1][:, 8 - (FFN_CONV_W - 1):]
    else:
        outs = _ffn_short(x, ffn_hist, lw, norm_f_g, final)
        ffn_new = outs[1]
    x = outs[0]
    xn = outs[2] if final else None
    state = (k.reshape(B, T, N_HEAD, D_HEAD), v.reshape(B, T, N_HEAD, D_HEAD), logf, conv_new, shift_new,
             wkv_new, ffn_new)
    return x, xn, state


def kernel(x_prompt, x_sample, cache_k, cache_v, cache_logf, state_conv, state_shift, state_wkv, state_ffn,
           page_table, norm1_g, w_in, b_f, b_gate, w_oa, conv_w, conv_b, conv_ln_g, conv_ln_b, w_ob,
           rwkv_mu, rwkv_w0, rwkv_w_lora_up, rwkv_a0, rwkv_a_lora_up, rwkv_g_lora_up, rwkv_k_k, rwkv_k_a,
           rwkv_r_k, rwkv_gn_g, rwkv_gn_b, w_oc, w_out, norm2_g, w_up_ffn, ffn_conv_w, w_down, norm_f_g):
    p = dict(norm1_g=norm1_g, w_in=w_in, b_f=b_f, b_gate=b_gate, w_oa=w_oa, conv_w=conv_w, conv_b=conv_b,
             conv_ln_g=conv_ln_g, conv_ln_b=conv_ln_b, w_ob=w_ob, rwkv_mu=rwkv_mu, rwkv_w0=rwkv_w0,
             rwkv_w_lora_up=rwkv_w_lora_up, rwkv_a0=rwkv_a0, rwkv_a_lora_up=rwkv_a_lora_up,
             rwkv_g_lora_up=rwkv_g_lora_up, rwkv_k_k=rwkv_k_k, rwkv_k_a=rwkv_k_a, rwkv_r_k=rwkv_r_k,
             rwkv_gn_g=rwkv_gn_g, rwkv_gn_b=rwkv_gn_b, w_oc=w_oc, w_out=w_out, norm2_g=norm2_g,
             w_up_ffn=w_up_ffn, ffn_conv_w=ffn_conv_w, w_down=w_down)
    depth = w_in.shape[0]
    nf = norm_f_g.reshape(1, D_MODEL)
    weights = [_layer_weights(l, p) for l in range(depth)]

    bp, tp, _ = x_prompt.shape
    bs, ts, _ = x_sample.shape
    page = cache_k.shape[2]

    yp, p_states = x_prompt, []
    for l in range(depth):
        attend = lambda qc, kc, vt: _fox_prompt(qc, kc, vt, _tile(tp, 512), _tile(tp, 1024))
        yp, ypn, st = _layer(yp, weights[l], attend,
                             jnp.zeros((bp, CONV_W - 1, W_HEADS), F32), jnp.zeros((bp, 1, C_SHIFT), F32),
                             jnp.zeros((bp, N_HEAD, D_HEAD, D_HEAD), F32), jnp.zeros((bp, FFN_CONV_W - 1, D_FF), F32),
                             nf, l == depth - 1, True)
        p_states.append(st)

    ys, s_states = x_sample, []
    n_pages = page_table.shape[1]
    n_group = 16 if n_pages % 16 == 0 else 1
    n_pool = cache_k.shape[1]
    kt_pool = jnp.transpose(cache_k, (0, 1, 3, 4, 2)).reshape(depth, n_pool, W_HEADS, page)
    vt_pool = jnp.transpose(cache_v, (0, 1, 3, 4, 2)).reshape(depth, n_pool, W_HEADS, page)
    lf_pool_t = jnp.swapaxes(cache_logf, 2, 3)
    for l in range(depth):

        def attend(q, k, v, logf, l=l):
            padk = lambda a: jnp.pad(a, ((0, 0), (0, page - ts), (0, 0)))
            lf_t = jnp.pad(jnp.swapaxes(logf, 1, 2), ((0, 0), (0, 0), (0, page - ts)))
            return _fox_sample(q, padk(k), padk(v), lf_t, l, kt_pool, vt_pool, lf_pool_t, page_table, n_group)

        ys, ysn, st = _layer(ys, weights[l], attend, state_conv[l], state_shift[l], state_wkv[l], state_ffn[l],
                             nf, l == depth - 1, False)
        s_states.append(st)

    stk = lambda states, i: jnp.stack([s[i] for s in states], axis=0)
    return (ypn, ysn,
            stk(p_states, 0), stk(p_states, 1), stk(p_states, 2), stk(p_states, 3),
            stk(p_states, 4), stk(p_states, 5), stk(p_states, 6),
            stk(s_states, 0), stk(s_states, 1), stk(s_states, 2), stk(s_states, 3),
            stk(s_states, 4), stk(s_states, 5), stk(s_states, 6))
```

```python
import functools

import numpy as np
import jax
import jax.numpy as jnp
from jax import lax
from jax.experimental import pallas as pl
from jax.experimental.pallas import tpu as pltpu

F32 = jnp.float32
BF16 = jnp.bfloat16

D_MODEL = 1024
N_HEAD = 8
D_HEAD = 64
W_HEADS = N_HEAD * D_HEAD
LANES = 128
N_PAIR = W_HEADS // LANES
C_SHIFT = 3 * W_HEADS + 64 + 64 + 128
D_FF = 2816
CONV_W = 31
FFN_CONV_W = 3
RMS_EPS = 1e-6
LN_EPS = 1e-5
GN_EPS = 64e-5
L2_EPS = 1e-12
NEG_BIG = -1e30
LOG2E = float(np.log2(np.e))
VMEM_LIMIT = 56 * 2**20


def _params(sem):
    return pltpu.CompilerParams(dimension_semantics=sem, vmem_limit_bytes=VMEM_LIMIT)


def _const_spec(shape):
    nd = len(shape)
    return pl.BlockSpec(shape, lambda *_: (0,) * nd, pipeline_mode=pl.Buffered(1))


def _dot(a, b):
    return jnp.dot(a.astype(BF16), b.astype(BF16), preferred_element_type=F32)


def _dot_nt(a, b):
    return lax.dot_general(a.astype(BF16), b.astype(BF16), (((1,), (1,)), ((), ())),
                           preferred_element_type=F32)


def _split3(x):
    hi = x.astype(BF16).astype(F32)
    r1 = x - hi
    mid = r1.astype(BF16).astype(F32)
    lo = (r1 - mid).astype(BF16).astype(F32)
    return hi, mid, lo


def _tri_dot(tri, x):
    pieces = jnp.concatenate(_split3(x), axis=1).astype(BF16)
    r = jnp.dot(tri, pieces, preferred_element_type=F32)
    return r[:, :LANES] + r[:, LANES:2 * LANES] + r[:, 2 * LANES:]


def _dot_tri(x, tri):
    pieces = jnp.concatenate(_split3(x), axis=0).astype(BF16)
    r = jnp.dot(pieces, tri, preferred_element_type=F32)
    n = x.shape[0]
    return r[:n] + r[n:2 * n] + r[2 * n:]


def _log_sigmoid(z):
    return jnp.minimum(z, 0.0) - jnp.log1p(jnp.exp(-jnp.abs(z)))


def _rmsnorm(x, g):
    return x * lax.rsqrt(jnp.mean(x * x, axis=-1, keepdims=True) + RMS_EPS) * g


def _inproj_body(prompt, x_ref, g_ref, wqkv_ref, wf_ref, bf_ref, wglu_ref, wr_ref, wg_ref, bg_ref, *rest):
    if prompt:
        (eplace_ref, qc_ref, kc_ref, k_ref, v_ref, vb_ref, lf_ref, u_ref, p_ref, gate_ref, fcarry) = rest
    else:
        (q_ref, k_ref, v_ref, lf_ref, u_ref, p_ref, gate_ref) = rest
    tm = x_ref.shape[1]
    hn = _rmsnorm(x_ref[0], g_ref[...]).astype(BF16)

    qkv = jnp.dot(hn, wqkv_ref[...], preferred_element_type=F32)
    q = qkv[:, :W_HEADS] * (D_HEAD ** -0.5 * (LOG2E if prompt else 1.0))
    k = qkv[:, W_HEADS:2 * W_HEADS]
    v = qkv[:, 2 * W_HEADS:]
    k_ref[0] = k
    v_ref[0] = v

    logf = _log_sigmoid(jnp.dot(hn, wf_ref[...], preferred_element_type=F32) + bf_ref[...])
    lf_ref[0] = logf[:, :N_HEAD]

    glu = jnp.dot(hn, wglu_ref[...], preferred_element_type=F32)
    u_ref[0] = glu[:, :W_HEADS] * jax.nn.sigmoid(glu[:, W_HEADS:])
    p_ref[0] = jnp.dot(hn, wr_ref[...], preferred_element_type=F32)
    gate_ref[0] = jax.nn.sigmoid(jnp.dot(hn, wg_ref[...], preferred_element_type=F32)
                                 + bg_ref[...]).astype(gate_ref.dtype)

    if not prompt:
        q_ref[0] = q.astype(q_ref.dtype)
        return

    v_t = v.T.astype(BF16)
    for pair in range(N_PAIR):
        vb_ref[0, pair] = v_t[pair * LANES:(pair + 1) * LANES, :]

    @pl.when(pl.program_id(1) == 0)
    def _():
        fcarry[...] = jnp.zeros_like(fcarry)

    lane = lax.broadcasted_iota(jnp.int32, (1, LANES), 1)
    lf = jnp.where(lane < N_HEAD, logf, 0.0)
    row = lax.broadcasted_iota(jnp.int32, (tm, tm), 0)
    col = lax.broadcasted_iota(jnp.int32, (tm, tm), 1)
    tri = jnp.where(col <= row, 1.0, 0.0).astype(BF16)
    fcum = _tri_dot(tri, lf) + fcarry[...]
    fcarry[...] = fcum[tm - 1:tm, :]
    aug = jnp.dot(jnp.concatenate(_split3(fcum * LOG2E), axis=1).astype(BF16), eplace_ref[...],
                  preferred_element_type=F32)
    slot = lane & 15
    augq = aug[:, :LANES] + jnp.where((slot >= 3) & (slot < 6), 1.0, 0.0)
    augk = aug[:, LANES:] + jnp.where(slot < 3, 1.0, 0.0)
    augk_b = augk.astype(BF16)
    for pair in range(N_PAIR):
        kc_ref[0, pair] = jnp.concatenate([k[:, pair * LANES:(pair + 1) * LANES].astype(BF16), augk_b], axis=1)
        q_pair = q[:, pair * LANES:(pair + 1) * LANES]
        for par in range(2):
            h = 2 * pair + par
            q_h = jnp.where((lane >= D_HEAD) if par else (lane < D_HEAD), q_pair, 0.0)
            a_h = jnp.where((lane >> 4) == h, augq, 0.0)
            qc_ref[0, h] = jnp.concatenate([q_h, a_h], axis=1).astype(BF16)


def _eplace():
    e = np.zeros((3 * LANES, 2 * LANES), np.float32)
    for piece in range(3):
        for h in range(N_HEAD):
            e[piece * LANES + h, 16 * h + piece] = 1.0
            e[piece * LANES + h, LANES + 16 * h + 3 + piece] = -1.0
    return jnp.asarray(e, BF16)


def _in_proj(x, lw, prompt, tm):
    B, T, _ = x.shape
    nt = T // tm
    row = lambda width: pl.BlockSpec((1, tm, width), lambda b, i: (b, i, 0))
    weights = [lw['norm1_g'], lw['w_qkv'], lw['w_f'], lw['b_f'], lw['w_glu'], lw['w_r'], lw['w_g'], lw['b_gate']]
    in_specs = [row(D_MODEL)] + [_const_spec(w.shape) for w in weights]
    common_shapes = [
        jax.ShapeDtypeStruct((B, T, W_HEADS), F32),
        jax.ShapeDtypeStruct((B, T, W_HEADS), F32),
    ]
    tail_shapes = [
        jax.ShapeDtypeStruct((B, T, N_HEAD), F32),
        jax.ShapeDtypeStruct((B, T, W_HEADS), F32),
        jax.ShapeDtypeStruct((B, T, C_SHIFT), F32),
        jax.ShapeDtypeStruct((B, T, 3 * D_MODEL), BF16),
    ]
    tail_specs = [row(N_HEAD), row(W_HEADS), row(C_SHIFT), row(3 * D_MODEL)]
    if prompt:
        args = [x] + weights + [_eplace()]
        in_specs = in_specs + [_const_spec((3 * LANES, 2 * LANES))]
        out_shape = [jax.ShapeDtypeStruct((B, N_HEAD, T, 2 * LANES), BF16),
                     jax.ShapeDtypeStruct((B, N_PAIR, T, 2 * LANES), BF16)] + common_shapes + \
                    [jax.ShapeDtypeStruct((B, N_PAIR, LANES, T), BF16)] + tail_shapes
        out_specs = [pl.BlockSpec((1, N_HEAD, tm, 2 * LANES), lambda b, i: (b, 0, i, 0)),
                     pl.BlockSpec((1, N_PAIR, tm, 2 * LANES), lambda b, i: (b, 0, i, 0)),
                     row(W_HEADS), row(W_HEADS),
                     pl.BlockSpec((1, N_PAIR, LANES, tm), lambda b, i: (b, 0, 0, i))] + tail_specs
        scratch = [pltpu.VMEM((1, LANES), F32)]
    else:
        args = [x] + weights
        out_shape = [jax.ShapeDtypeStruct((B, T, W_HEADS), BF16)] + common_shapes + tail_shapes
        out_specs = [row(W_HEADS), row(W_HEADS), row(W_HEADS)] + tail_specs
        scratch = []
    return pl.pallas_call(
        functools.partial(_inproj_body, prompt),
        grid=(B, nt), in_specs=in_specs, out_specs=out_specs, out_shape=out_shape,
        scratch_shapes=scratch, compiler_params=_params(("arbitrary", "arbitrary")),
        name="in_proj_prompt" if prompt else "in_proj_sample",
    )(*args)


ONES_ROWS = 16


def _fox_prompt_body(qc_ref, kc_ref, vt_ref, o_ref, v1t, *, tq, tk):
    i = pl.program_id(2)
    n_pp = kc_ref.shape[1]
    heads = range(2 * n_pp)
    n_kt = v1t.shape[1]

    @pl.when(i == 0)
    def _():
        for pp in range(n_pp):
            for j in range(n_kt):
                v1t[pp, j, :LANES, :] = vt_ref[0, pp, :, j * tk:(j + 1) * tk]
                v1t[pp, j, LANES:, :] = jnp.ones((ONES_ROWS, tk), BF16)

    qs = [qc_ref[0, h] for h in heads]

    def tile(j, carry, masked):
        start = pl.multiple_of(j * tk, tk)
        kcs = [kc_ref[0, pp, pl.ds(start, tk), :] for pp in range(n_pp)]
        vts = [v1t[pp, j] for pp in range(n_pp)]
        ss = [lax.dot_general(kcs[h // 2], qs[h], (((1,), (1,)), ((), ())), preferred_element_type=F32)
              for h in heads]
        if masked:
            key = start + lax.broadcasted_iota(jnp.int32, (tk, tq), 0)
            qpos = i * tq + lax.broadcasted_iota(jnp.int32, (tk, tq), 1)
            ss = [jnp.where(key <= qpos, s, NEG_BIG) for s in ss]
        out = []
        for pp in range(n_pp):
            pair = (2 * pp, 2 * pp + 1)
            m_new = [jnp.maximum(carry[h][0], jnp.max(ss[h], axis=0, keepdims=True)) for h in pair]
            ps = [jnp.exp2(ss[h] - m).astype(BF16) for h, m in zip(pair, m_new)]
            accs = [jnp.exp2(carry[h][0] - m) * carry[h][1] + jnp.dot(vts[pp], p, preferred_element_type=F32)
                    for h, m, p in zip(pair, m_new, ps)]
            out.extend(zip(m_new, accs))
        return tuple(out)

    init = tuple((jnp.full((1, tq), NEG_BIG, F32), jnp.zeros((LANES + ONES_ROWS, tq), F32)) for _ in heads)
    n_full = (i * tq) // tk
    carry = lax.fori_loop(0, n_full, lambda j, c: tile(j, c, False), init)
    final = tile(n_full, carry, True)
    chan = lax.broadcasted_iota(jnp.int32, (LANES, 1), 0)
    for pp in range(n_pp):
        a0, a1 = final[2 * pp][1], final[2 * pp + 1][1]
        o_t = jnp.where(chan < D_HEAD, a0[:LANES] / a0[LANES:LANES + 1], a1[:LANES] / a1[LANES:LANES + 1])
        o_ref[0, :, pp * LANES:(pp + 1) * LANES] = o_t.T.astype(o_ref.dtype)


def _fox_prompt(qc, kc, vt, tq, tk, n_pp=2):
    B, _, T, _ = qc.shape
    return pl.pallas_call(
        functools.partial(_fox_prompt_body, tq=tq, tk=tk),
        grid=(B, N_PAIR // n_pp, T // tq),
        in_specs=[pl.BlockSpec((1, 2 * n_pp, tq, 2 * LANES), lambda b, p, i: (b, p, i, 0)),
                  pl.BlockSpec((1, n_pp, T, 2 * LANES), lambda b, p, i: (b, p, 0, 0)),
                  pl.BlockSpec((1, n_pp, LANES, T), lambda b, p, i: (b, p, 0, 0))],
        out_specs=pl.BlockSpec((1, tq, n_pp * LANES), lambda b, p, i: (b, i, p)),
        out_shape=jax.ShapeDtypeStruct((B, T, W_HEADS), BF16),
        scratch_shapes=[pltpu.VMEM((n_pp, T // tk, LANES + ONES_ROWS, tk), BF16)],
        compiler_params=_params(("arbitrary", "arbitrary", "arbitrary")),
        name="fox_prompt",
    )(qc, kc, vt)


def _fox_sample_body(n_group, t_new, pt_ref, q_ref, kn_ref, vn_ref, lfn_ref, *rest):
    k_refs = rest[:n_group]
    v_refs = rest[n_group:2 * n_group]
    lf_refs = rest[2 * n_group:3 * n_group]
    o_ref, qbd, m_s, l_s, acc_s, tail_s, newsum_s, k_all, v_all = rest[3 * n_group:]
    j = pl.program_id(1)
    nrow = t_new * N_HEAD
    page = lf_refs[0].shape[1]
    lane_w = lax.broadcasted_iota(jnp.int32, (N_HEAD, W_HEADS), 1)
    head_w = lax.broadcasted_iota(jnp.int32, (N_HEAD, W_HEADS), 0)
    bd_mask = (lane_w >> 6) == head_w
    ri = lax.broadcasted_iota(jnp.int32, (page, page), 0)
    ci = lax.broadcasted_iota(jnp.int32, (page, page), 1)

    def tile4(x):
        return jnp.concatenate([x] * t_new, axis=0)

    def online(s, pv_fn):
        m_new = jnp.maximum(m_s[...], jnp.max(s, axis=-1, keepdims=True))
        alpha = jnp.exp(m_s[...] - m_new)
        p = jnp.exp(s - m_new)
        l_s[...] = alpha * l_s[...] + jnp.sum(p, axis=-1, keepdims=True)
        acc_s[...] = alpha * acc_s[...] + pv_fn(p.astype(BF16))
        m_s[...] = m_new

    @pl.when(j == 0)
    def _():
        q = q_ref[0].astype(F32)
        qbd[...] = jnp.concatenate(
            [jnp.where(bd_mask, jnp.broadcast_to(q[t:t + 1, :], (N_HEAD, W_HEADS)), 0.0)
             for t in range(t_new)], axis=0).astype(BF16)
        m_s[...] = jnp.full_like(m_s, NEG_BIG)
        l_s[...] = jnp.zeros_like(l_s)
        acc_s[...] = jnp.zeros_like(acc_s)
        tail_s[...] = jnp.zeros_like(tail_s)
        tri_incl = jnp.where(ri <= ci, 1.0, 0.0).astype(BF16)
        cum = _dot_tri(lfn_ref[0], tri_incl)
        newsum = jnp.concatenate([cum[:, t:t + 1] for t in range(t_new)], axis=0)
        newsum_s[...] = newsum
        s = _dot_nt(qbd[...], kn_ref[0])
        rr = lax.broadcasted_iota(jnp.int32, (nrow, page), 0) >> 3
        cc = lax.broadcasted_iota(jnp.int32, (nrow, page), 1)
        s = jnp.where(cc <= rr, s + newsum - tile4(cum), NEG_BIG)
        online(s, lambda p: _dot(p, vn_ref[0]))

    tri_after = jnp.where(ri > ci, 1.0, 0.0).astype(BF16)
    lfs = [lf_refs[g][...] for g in range(n_group)]
    pieces = jnp.concatenate([jnp.concatenate(_split3(lf), axis=0) for lf in lfs], axis=0).astype(BF16)
    suffix = jnp.dot(pieces, tri_after, preferred_element_type=F32)
    qb = qbd[...]
    tail = tail_s[...]
    parts = []
    for g in range(n_group):
        r0 = 3 * N_HEAD * g
        bias8 = (suffix[r0:r0 + N_HEAD] + suffix[r0 + N_HEAD:r0 + 2 * N_HEAD]
                 + suffix[r0 + 2 * N_HEAD:r0 + 3 * N_HEAD] + tail)
        parts.append(tile4(bias8))
        tail = tail + jnp.sum(lfs[g], axis=-1, keepdims=True)
        k_all[:, g * page:(g + 1) * page] = k_refs[g][...].astype(BF16)
        v_all[:, g * page:(g + 1) * page] = v_refs[g][...].astype(BF16)
    tail_s[...] = tail
    s_all = (jnp.dot(qb, k_all[...], preferred_element_type=F32) + jnp.concatenate(parts, axis=1)
             + newsum_s[...])
    online(s_all, lambda p: _dot_nt(p, v_all[...]))

    @pl.when(j == pl.num_programs(1) - 1)
    def _():
        o = acc_s[...] / l_s[...]
        for t in range(t_new):
            o_ref[0, t:t + 1, :] = jnp.sum(jnp.where(bd_mask, o[t * N_HEAD:(t + 1) * N_HEAD, :], 0.0), axis=0,
                                           keepdims=True).astype(o_ref.dtype)


def _fox_sample(q, k_new, v_new, lf_new_t, layer, kt_pool, vt_pool, lf_pool_t, page_table, n_group):
    B, t_new, _ = q.shape
    page = kt_pool.shape[3]
    n_pages = page_table.shape[1]
    steps = n_pages // n_group

    def paged(width_shape):
        specs = []
        for g in range(n_group):
            specs.append(pl.BlockSpec((None, None) + width_shape,
                                      lambda b, j, pt, g=g: (layer, pt[b, n_pages - 1 - (j * n_group + g)], 0, 0)))
        return specs

    per_b = lambda shape: pl.BlockSpec((1,) + shape, lambda b, j, pt: (b, 0, 0))
    nrow = t_new * N_HEAD
    grid_spec = pltpu.PrefetchScalarGridSpec(
        num_scalar_prefetch=1, grid=(B, steps),
        in_specs=[per_b((t_new, W_HEADS)), per_b((page, W_HEADS)), per_b((page, W_HEADS)), per_b((N_HEAD, page))]
        + paged((W_HEADS, page)) + paged((W_HEADS, page)) + paged((N_HEAD, page)),
        out_specs=per_b((t_new, W_HEADS)),
        scratch_shapes=[pltpu.VMEM((nrow, W_HEADS), BF16), pltpu.VMEM((nrow, 1), F32), pltpu.VMEM((nrow, 1), F32),
                        pltpu.VMEM((nrow, W_HEADS), F32), pltpu.VMEM((N_HEAD, 1), F32), pltpu.VMEM((nrow, 1), F32),
                        pltpu.VMEM((W_HEADS, n_group * page), BF16), pltpu.VMEM((W_HEADS, n_group * page), BF16)])
    return pl.pallas_call(
        functools.partial(_fox_sample_body, n_group, t_new),
        grid_spec=grid_spec,
        out_shape=jax.ShapeDtypeStruct((B, t_new, W_HEADS), F32),
        compiler_params=_params(("arbitrary", "arbitrary")),
        name="fox_sample",
    )(page_table, q, k_new, v_new, lf_new_t, *([kt_pool] * n_group), *([vt_pool] * n_group),
      *([lf_pool_t] * n_group))


HIST_ROWS = 32


SUBLANES = 8
CONV_ROWS = 32


def _conv_body(u_ref, hist_ref, w_ref, b_ref, g_ref, beta_ref, o_ref, ext, shifted):
    tt = u_ref.shape[1]

    @pl.when(pl.program_id(1) == 0)
    def _():
        ext[0:HIST_ROWS, :] = hist_ref[0]

    ext[HIST_ROWS:HIST_ROWS + tt, :] = u_ref[0]
    span = tt + HIST_ROWS - SUBLANES
    for s in range(1, SUBLANES):
        shifted[s - 1, 0:span, :] = ext[s:s + span, :]
    first = HIST_ROWS - (CONV_W - 1)
    rows = min(tt, CONV_ROWS)
    for c0 in range(0, tt, rows):
        acc = jnp.zeros((rows, W_HEADS), F32)
        for tap in range(CONV_W):
            s = (first + tap) % SUBLANES
            base = first + tap - s + c0
            win = ext[base:base + rows, :] if s == 0 else shifted[s - 1, base:base + rows, :]
            acc = acc + w_ref[tap:tap + 1, :] * win
        c = acc + b_ref[...]
        mean = jnp.mean(c, axis=-1, keepdims=True)
        xc = c - mean
        var = jnp.mean(xc * xc, axis=-1, keepdims=True)
        y = xc * lax.rsqrt(var + LN_EPS) * g_ref[...] + beta_ref[...]
        o_ref[0, c0:c0 + rows, :] = (y * jax.nn.sigmoid(y)).astype(o_ref.dtype)
    ext[0:HIST_ROWS, :] = ext[tt:tt + HIST_ROWS, :]


def _conv(u, hist, lw, tt):
    B, T, _ = u.shape
    return pl.pallas_call(
        _conv_body, grid=(B, T // tt),
        in_specs=[pl.BlockSpec((1, tt, W_HEADS), lambda b, i: (b, i, 0)),
                  pl.BlockSpec((1, HIST_ROWS, W_HEADS), lambda b, i: (b, 0, 0)),
                  _const_spec((CONV_W, W_HEADS)), _const_spec((1, W_HEADS)), _const_spec((1, W_HEADS)),
                  _const_spec((1, W_HEADS))],
        out_specs=pl.BlockSpec((1, tt, W_HEADS), lambda b, i: (b, i, 0)),
        out_shape=jax.ShapeDtypeStruct((B, T, W_HEADS), BF16),
        scratch_shapes=[pltpu.VMEM((HIST_ROWS + tt, W_HEADS), F32),
                        pltpu.VMEM((SUBLANES - 1, HIST_ROWS + tt - SUBLANES, W_HEADS), F32)],
        compiler_params=_params(("arbitrary", "arbitrary")),
        name="conv",
    )(u, hist, lw['conv_w'], lw['conv_b'], lw['conv_ln_g'], lw['conv_ln_b'])


def _rwkv_prep_body(p_ref, hist_ref, mu_ref, w0_ref, a0_ref, kk_ref, ka_ref, rk_ref, wwa_ref, wg_ref, bd_ref,
                    r_o, lw_o, k_o, v_o, kk_o, b_o, g_o, bonus_o, ext):
    tt = p_ref.shape[1]

    @pl.when(pl.program_id(1) == 0)
    def _():
        ext[0:8, :] = hist_ref[0]

    p = p_ref[0]
    ext[8:8 + tt, :] = p
    prev = ext[7:7 + tt, :]
    ps = p + (prev - p) * mu_ref[...]
    ext[0:8, :] = ext[tt:tt + 8, :]

    r = ps[:, :W_HEADS]
    kc = ps[:, W_HEADS:2 * W_HEADS]
    vc = ps[:, 2 * W_HEADS:3 * W_HEADS]
    wa = ps[:, 3 * W_HEADS:3 * W_HEADS + LANES]
    gd = ps[:, 3 * W_HEADS + LANES:]
    lane = lax.broadcasted_iota(jnp.int32, (1, LANES), 1)
    wa = jnp.where(lane < 64, jnp.tanh(wa), wa)
    lora = _dot(wa, wwa_ref[...])
    lw_o[0] = -float(np.exp(-0.5)) * jax.nn.sigmoid(w0_ref[...] + lora[:, :W_HEADS])
    a = jax.nn.sigmoid(a0_ref[...] + lora[:, W_HEADS:])
    g_o[0] = _dot(jax.nn.sigmoid(gd), wg_ref[...])
    bd = bd_ref[...]
    kk = kc * kk_ref[...]
    sq = kk * kk
    sq_hi = sq.astype(BF16)
    sq_lo = (sq - sq_hi.astype(F32)).astype(BF16)
    ss = jnp.dot(sq_hi, bd, preferred_element_type=F32) + jnp.dot(sq_lo, bd, preferred_element_type=F32)
    kk = kk / jnp.maximum(jnp.sqrt(ss), L2_EPS)
    k_eff = kc * (1.0 + (a - 1.0) * ka_ref[...])
    r_o[0] = r
    k_o[0] = k_eff
    v_o[0] = vc
    kk_o[0] = kk
    b_o[0] = kk * a
    bonus_o[0] = _dot(r * k_eff * rk_ref[...], bd) * vc


def _rwkv_prep(p, hist, lw, tt):
    B, T, _ = p.shape
    row = lambda width: pl.BlockSpec((1, tt, width), lambda b, i: (b, i, 0))
    consts = [lw['rwkv_mu'], lw['rwkv_w0'], lw['rwkv_a0'], lw['rwkv_k_k'], lw['rwkv_k_a'], lw['rwkv_r_k'],
              lw['w_wa'], lw['rwkv_g_lora_up'], lw['bd_ones']]
    return pl.pallas_call(
        _rwkv_prep_body, grid=(B, T // tt),
        in_specs=[row(C_SHIFT), pl.BlockSpec((1, 8, C_SHIFT), lambda b, i: (b, 0, 0))]
        + [_const_spec(c.shape) for c in consts],
        out_specs=[row(W_HEADS)] * 8,
        out_shape=[jax.ShapeDtypeStruct((B, T, W_HEADS), F32)] * 8,
        scratch_shapes=[pltpu.VMEM((8 + tt, C_SHIFT), F32)],
        compiler_params=_params(("arbitrary", "arbitrary")),
        name="rwkv_prep",
    )(p, hist, *consts)


def _rwkv_scan_body(r_ref, lw_ref, k_ref, v_ref, kk_ref, b_ref, a0_ref, y_ref, a_out_ref, a_s, *, chunk, nb):
    c = chunk
    n = 2 * c

    @pl.when(pl.program_id(1) == 0)
    def _():
        a_s[...] = a0_ref[...]

    ri = lax.broadcasted_iota(jnp.int32, (n, n), 0)
    ci = lax.broadcasted_iota(jnp.int32, (n, n), 1)
    strict = ri > ci
    incl = ri >= ci
    eye_f = jnp.where(ri == ci, 1.0, 0.0)
    tr = lax.broadcasted_iota(jnp.int32, (c, c), 0)
    tc = lax.broadcasted_iota(jnp.int32, (c, c), 1)
    tri_incl = jnp.where(tc <= tr, 1.0, 0.0).astype(BF16)
    lane = lax.broadcasted_iota(jnp.int32, (1, LANES), 1)
    m_even = lane < D_HEAD

    def stack(x):
        return jnp.concatenate([jnp.where(m_even, x, 0.0), jnp.where(m_even, 0.0, x)], axis=0)

    units = [(bi, pair) for bi in range(nb) for pair in range(N_PAIR)]

    def load(ref):
        return [ref[bi, :, pair * LANES:(pair + 1) * LANES] for bi, pair in units]

    lw = load(lw_ref)
    cum = [_tri_dot(tri_incl, x) for x in lw]
    cum_end = [x[c - 1:c, :] for x in cum]
    e_neg = [jnp.exp(-x) for x in cum]
    e_end = [jnp.exp(ce - x) for ce, x in zip(cum_end, cum)]
    r, k, v, kk, b = load(r_ref), load(k_ref), load(v_ref), load(kk_ref), load(b_ref)
    rt = [stack(x * jnp.exp(cu)).astype(BF16) for x, cu in zip(r, cum)]
    ka = [stack(x * jnp.exp(cu - l)).astype(BF16) for x, cu, l in zip(kk, cum, lw)]
    kt = [stack(x * e).astype(BF16) for x, e in zip(k, e_neg)]
    bt = [stack(x * e).astype(BF16) for x, e in zip(b, e_neg)]
    kh_t = [stack(x * e).T.astype(BF16) for x, e in zip(k, e_end)]
    bh_t = [stack(x * e).T.astype(BF16) for x, e in zip(b, e_end)]
    vs = [stack(x).astype(BF16) for x in v]

    big = [_dot_nt(jnp.concatenate([a, q], axis=0), jnp.concatenate([x, y], axis=0))
           for a, q, x, y in zip(ka, rt, kt, bt)]
    m_mat = [jnp.where(strict, x[:n, :n], 0.0).astype(BF16) for x in big]
    l_mat = [jnp.where(strict, x[:n, n:], 0.0) for x in big]
    n1 = [jnp.where(incl, x[n:, :n], 0.0).astype(BF16) for x in big]
    n2 = [jnp.where(incl, x[n:, n:], 0.0).astype(BF16) for x in big]

    inv = [eye_f - jnp.where((ri >> 1) == (ci >> 1), x, 0.0) for x in l_mat]
    size = 2
    shift = 1
    while size < c:
        blk = ((ri >> (shift + 1)) == (ci >> (shift + 1))) & (((ri >> shift) & 1) == 1) & (((ci >> shift) & 1) == 0)
        inv_b = [x.astype(BF16) for x in inv]
        half = [_dot(x, jnp.where(blk, l, 0.0)) for x, l in zip(inv_b, l_mat)]
        inv = [x - _dot(h, xb) for x, h, xb in zip(inv, half, inv_b)]
        size *= 2
        shift += 1
    inv_b = [x.astype(BF16) for x in inv]

    mv = [_dot(m, x).astype(BF16) for m, x in zip(m_mat, vs)]
    wu = [_dot(i, jnp.concatenate([a, x], axis=1)).astype(BF16) for i, a, x in zip(inv_b, ka, mv)]
    nw = [_dot(x, w) for x, w in zip(n2, wu)]
    rq = [q.astype(F32) - x[:, :LANES] for q, x in zip(rt, nw)]
    y0 = [_dot(m, x) - w[:, LANES:] for m, x, w in zip(n1, vs, nw)]
    bw = [_dot(x, w) for x, w in zip(bh_t, wu)]
    eye_l = lax.broadcasted_iota(jnp.int32, (LANES, LANES), 0) == lax.broadcasted_iota(jnp.int32, (LANES, LANES), 1)
    g_mat = [jnp.where(eye_l, jnp.broadcast_to(jnp.exp(ce), (LANES, LANES)), 0.0) - x[:, :LANES]
             for ce, x in zip(cum_end, bw)]
    h_mat = [_dot(x, y) - w[:, LANES:] for x, y, w in zip(kh_t, vs, bw)]
    a_prev = [a_s[bi, pair].astype(BF16) for bi, pair in units]
    ys = [_dot(q, a) + y for q, a, y in zip(rq, a_prev, y0)]
    a_new = [_dot(g, a) + h for g, a, h in zip(g_mat, a_prev, h_mat)]
    for (bi, pair), y, a in zip(units, ys, a_new):
        y_ref[bi, :, pair * LANES:(pair + 1) * LANES] = y[:c] + y[c:]
        a_s[bi, pair] = a

    @pl.when(pl.program_id(1) == pl.num_programs(1) - 1)
    def _():
        a_out_ref[...] = a_s[...]


def _rwkv_scan(r, lwd, k, v, kk, b, a0, chunk, nb):
    B, T, _ = r.shape
    row = pl.BlockSpec((nb, chunk, W_HEADS), lambda bb, i: (bb, i, 0))
    st = pl.BlockSpec((nb, N_PAIR, LANES, LANES), lambda bb, i: (bb, 0, 0, 0))
    return pl.pallas_call(
        functools.partial(_rwkv_scan_body, chunk=chunk, nb=nb), grid=(B // nb, T // chunk),
        in_specs=[row] * 6 + [st], out_specs=[row, st],
        out_shape=[jax.ShapeDtypeStruct((B, T, W_HEADS), F32),
                   jax.ShapeDtypeStruct((B, N_PAIR, LANES, LANES), F32)],
        scratch_shapes=[pltpu.VMEM((nb, N_PAIR, LANES, LANES), F32)],
        compiler_params=_params(("arbitrary", "arbitrary")),
        name="rwkv_scan",
    )(r, lwd, k, v, kk, b, a0)


def _merge_body(x_ref, oa_ref, ob_ref, y_ref, bonus_ref, g_ref, gate_ref, woa_ref, wob_ref, woc_ref, wout_ref,
                gng_ref, gnb_ref, bd_ref, o_ref):
    bd = bd_ref[...]
    y = y_ref[0]
    mean = _dot(y, bd) * (1.0 / D_HEAD)
    xc = y - mean
    var = _dot(xc * xc, bd) * (1.0 / D_HEAD)
    yn = xc * lax.rsqrt(var + GN_EPS) * gng_ref[...] + gnb_ref[...]
    oc = (yn + bonus_ref[0]) * g_ref[0]
    gates = gate_ref[0].astype(F32)
    merged = (gates[:, :D_MODEL] * _dot(oa_ref[0], woa_ref[...])
              + gates[:, D_MODEL:2 * D_MODEL] * _dot(ob_ref[0], wob_ref[...])
              + gates[:, 2 * D_MODEL:] * _dot(oc, woc_ref[...]))
    o_ref[0] = x_ref[0] + _dot(merged, wout_ref[...])


def _merge(x, oa, ob, y, bonus, g, gates, lw, tm):
    B, T, _ = x.shape
    row = lambda width: pl.BlockSpec((1, tm, width), lambda b, i: (b, i, 0))
    consts = [lw['w_oa'], lw['w_ob'], lw['w_oc'], lw['w_out'], lw['rwkv_gn_g'], lw['rwkv_gn_b'], lw['bd_ones']]
    return pl.pallas_call(
        _merge_body, grid=(B, T // tm),
        in_specs=[row(D_MODEL), row(W_HEADS), row(W_HEADS), row(W_HEADS), row(W_HEADS), row(W_HEADS),
                  row(3 * D_MODEL)] + [_const_spec(c.shape) for c in consts],
        out_specs=row(D_MODEL), out_shape=jax.ShapeDtypeStruct((B, T, D_MODEL), F32),
        compiler_params=_params(("arbitrary", "arbitrary")),
        name="merge",
    )(x, oa, ob, y, bonus, g, gates, *consts)


def _gelu_tanh(x):
    return 0.5 * x * (1.0 + jnp.tanh(float(np.sqrt(2.0 / np.pi)) * (x + 0.044715 * (x * x * x))))


def _ffn_body(final, x_ref, hist_ref, g2_ref, wa_ref, wb_ref, cw_ref, wd_ref, gf_ref, *rest):
    if final:
        o_ref, new_ref, yn_ref, ext = rest
    else:
        o_ref, new_ref, ext = rest
    tm = x_ref.shape[1]

    @pl.when(pl.program_id(1) == 0)
    def _():
        ext[0:8, :] = hist_ref[0]

    x = x_ref[0]
    hn = _rmsnorm(x, g2_ref[...]).astype(BF16)
    ext[8:8 + tm, :] = jnp.dot(hn, wa_ref[...], preferred_element_type=F32)
    ub = jnp.dot(hn, wb_ref[...], preferred_element_type=F32)
    c2 = (cw_ref[0:1, :] * ext[6:6 + tm, :] + cw_ref[1:2, :] * ext[7:7 + tm, :]
          + cw_ref[2:3, :] * ext[8:8 + tm, :])
    out = x + _dot(_gelu_tanh(c2) * ub, wd_ref[...])
    o_ref[0] = out
    last = ext[tm:tm + 8, :]
    new_ref[0] = last
    ext[0:8, :] = last
    if final:
        yn_ref[0] = _rmsnorm(out, gf_ref[...])


def _ffn_short_body(final, t_seq, x_ref, prev_ref, g2_ref, wa_ref, wb_ref, cw_ref, wd_ref, gf_ref, *rest):
    if final:
        o_ref, ua_ref, yn_ref, ext = rest
    else:
        o_ref, ua_ref, ext = rest
    tm = x_ref.shape[1]
    x = x_ref[0]
    hn = _rmsnorm(x, g2_ref[...]).astype(BF16)
    ua = jnp.dot(hn, wa_ref[...], preferred_element_type=F32)
    ub = jnp.dot(hn, wb_ref[...], preferred_element_type=F32)
    ua_ref[0] = ua
    ext[0:8, :] = jnp.zeros((8, D_FF), F32)
    ext[8:8 + tm, :] = ua
    t = lax.rem(lax.broadcasted_iota(jnp.int32, (tm, 1), 0), t_seq)
    back1 = jnp.where(t >= 1, ext[7:7 + tm, :], prev_ref[0])
    back2 = jnp.where(t >= 2, ext[6:6 + tm, :], prev_ref[1])
    c2 = cw_ref[0:1, :] * back2 + cw_ref[1:2, :] * back1 + cw_ref[2:3, :] * ua
    out = x + _dot(_gelu_tanh(c2) * ub, wd_ref[...])
    o_ref[0] = out
    if final:
        yn_ref[0] = _rmsnorm(out, gf_ref[...])


def _ffn_short(x, hist, lw, norm_f_g, final):
    B, T, _ = x.shape
    M = B * T
    zeros = jnp.zeros((B, T, D_FF), F32)
    prev1 = zeros.at[:, 0].set(hist[:, 1])
    prev2 = zeros.at[:, 0].set(hist[:, 0]).at[:, 1].set(hist[:, 1])
    prev = jnp.stack([prev1.reshape(M, D_FF), prev2.reshape(M, D_FF)])
    row = _const_spec((1, M, D_MODEL))
    wide = _const_spec((1, M, D_FF))
    consts = [lw['norm2_g'], lw['w_up_a'], lw['w_up_b'], lw['ffn_conv_w'], lw['w_down'], norm_f_g]
    out_specs = [row, wide]
    out_shape = [jax.ShapeDtypeStruct((1, M, D_MODEL), F32), jax.ShapeDtypeStruct((1, M, D_FF), F32)]
    if final:
        out_specs.append(row)
        out_shape.append(jax.ShapeDtypeStruct((1, M, D_MODEL), F32))
    outs = pl.pallas_call(
        functools.partial(_ffn_short_body, final, T), grid=(1,),
        in_specs=[row, _const_spec((2, M, D_FF))] + [_const_spec(c.shape) for c in consts],
        out_specs=out_specs, out_shape=out_shape,
        scratch_shapes=[pltpu.VMEM((8 + M, D_FF), F32)],
        compiler_params=_params(("arbitrary",)),
        name="ffn_short",
    )(x.reshape(1, M, D_MODEL), prev, *consts)
    res = [outs[0].reshape(B, T, D_MODEL), outs[1].reshape(B, T, D_FF)[:, T - (FFN_CONV_W - 1):]]
    if final:
        res.append(outs[2].reshape(B, T, D_MODEL))
    return res


def _ffn(x, hist, lw, norm_f_g, final, tm):
    B, T, _ = x.shape
    row = pl.BlockSpec((1, tm, D_MODEL), lambda b, i: (b, i, 0))
    hist_spec = pl.BlockSpec((1, 8, D_FF), lambda b, i: (b, 0, 0))
    consts = [lw['norm2_g'], lw['w_up_a'], lw['w_up_b'], lw['ffn_conv_w'], lw['w_down'], norm_f_g]
    out_specs = [row, hist_spec]
    out_shape = [jax.ShapeDtypeStruct((B, T, D_MODEL), F32), jax.ShapeDtypeStruct((B, 8, D_FF), F32)]
    if final:
        out_specs.append(row)
        out_shape.append(jax.ShapeDtypeStruct((B, T, D_MODEL), F32))
    return pl.pallas_call(
        functools.partial(_ffn_body, final), grid=(B, T // tm),
        in_specs=[row, hist_spec] + [_const_spec(c.shape) for c in consts],
        out_specs=out_specs, out_shape=out_shape,
        scratch_shapes=[pltpu.VMEM((8 + tm, D_FF), F32)],
        compiler_params=_params(("arbitrary", "arbitrary")),
        name="ffn",
    )(x, hist, *consts)


def _layer_weights(l, p):
    w_in = p['w_in'][l]
    o_f = 3 * W_HEADS
    o_glu = o_f + N_HEAD
    o_r = o_glu + 2 * W_HEADS
    o_g = o_r + C_SHIFT
    row = lambda a: a[l].reshape(1, -1)
    wwa = jnp.zeros((LANES, 2 * W_HEADS), F32)
    wwa = wwa.at[:64, :W_HEADS].set(p['rwkv_w_lora_up'][l]).at[64:, W_HEADS:].set(p['rwkv_a_lora_up'][l])
    head = np.arange(W_HEADS) // D_HEAD
    return {
        'norm1_g': row(p['norm1_g']),
        'w_qkv': w_in[:, :o_f].astype(BF16),
        'w_f': jnp.pad(w_in[:, o_f:o_glu], ((0, 0), (0, LANES - N_HEAD))).astype(BF16),
        'b_f': jnp.pad(p['b_f'][l], (0, LANES - N_HEAD)).reshape(1, LANES),
        'w_glu': w_in[:, o_glu:o_r].astype(BF16),
        'w_r': w_in[:, o_r:o_g].astype(BF16),
        'w_g': w_in[:, o_g:].astype(BF16),
        'b_gate': row(p['b_gate']),
        'conv_w': p['conv_w'][l], 'conv_b': row(p['conv_b']), 'conv_ln_g': row(p['conv_ln_g']),
        'conv_ln_b': row(p['conv_ln_b']),
        'rwkv_mu': row(p['rwkv_mu']), 'rwkv_w0': row(p['rwkv_w0']), 'rwkv_a0': row(p['rwkv_a0']),
        'rwkv_k_k': row(p['rwkv_k_k']), 'rwkv_k_a': row(p['rwkv_k_a']), 'rwkv_r_k': row(p['rwkv_r_k']),
        'w_wa': wwa.astype(BF16), 'rwkv_g_lora_up': p['rwkv_g_lora_up'][l].astype(BF16),
        'bd_ones': jnp.asarray(head[:, None] == head[None, :], BF16),
        'rwkv_gn_g': row(p['rwkv_gn_g']), 'rwkv_gn_b': row(p['rwkv_gn_b']),
        'w_oa': p['w_oa'][l].astype(BF16), 'w_ob': p['w_ob'][l].astype(BF16), 'w_oc': p['w_oc'][l].astype(BF16),
        'w_out': p['w_out'][l].astype(BF16),
        'norm2_g': row(p['norm2_g']),
        'w_up_a': p['w_up_ffn'][l][:, :D_FF].astype(BF16), 'w_up_b': p['w_up_ffn'][l][:, D_FF:].astype(BF16),
        'ffn_conv_w': p['ffn_conv_w'][l], 'w_down': p['w_down'][l].astype(BF16),
    }


def _pad_rows_front(a, rows):
    return jnp.pad(a, ((0, 0), (rows - a.shape[1], 0), (0, 0)))


def _state_to_pairs(s):
    B = s.shape[0]
    a = jnp.swapaxes(s, -1, -2).reshape(B, N_PAIR, 2, D_HEAD, D_HEAD)
    z = jnp.zeros_like(a[:, :, 0])
    top = jnp.concatenate([a[:, :, 0], z], axis=-1)
    bot = jnp.concatenate([z, a[:, :, 1]], axis=-1)
    return jnp.concatenate([top, bot], axis=-2)


def _pairs_to_state(a):
    B = a.shape[0]
    even = a[:, :, :D_HEAD, :D_HEAD]
    odd = a[:, :, D_HEAD:, D_HEAD:]
    s = jnp.stack([even, odd], axis=2).reshape(B, N_HEAD, D_HEAD, D_HEAD)
    return jnp.swapaxes(s, -1, -2)


def _tile(t, pref):
    return pref if t % pref == 0 else t


def _layer(x, lw, attend, conv_hist, shift_hist, wkv0, ffn_hist, norm_f_g, final, prompt):
    B, T, _ = x.shape
    if prompt:
        qc, kc, k, v, vb, logf, u, p, gates = _in_proj(x, lw, True, _tile(T, 512))
        o_a = attend(qc, kc, vb)
        seq = lambda a: a
        unseq = lambda a: a
    else:
        flat = x.reshape(1, B * T, D_MODEL)
        q, k, v, logf, u, p, gates = [a.reshape((B, T) + a.shape[2:]) for a in _in_proj(flat, lw, False, B * T)]
        o_a = attend(q, k, v, logf)
        seq = lambda a: a
        unseq = lambda a: a
    if T >= CONV_W - 1:
        conv_new = u[:, T - (CONV_W - 1):]
    else:
        conv_new = jnp.concatenate([conv_hist[:, T:], u], axis=1)
    o_b = _conv(u, _pad_rows_front(conv_hist, HIST_ROWS), lw, _tile(T, 256))
    shift_new = p[:, -1:]
    r, lwd, k_eff, vc, kk, beta, g, bonus = _rwkv_prep(p, _pad_rows_front(shift_hist, 8), lw, _tile(T, 256))
    chunk = 64
    nb = 2 if B % 2 == 0 else 1
    if T % chunk:
        pad = lambda a: jnp.pad(a, ((0, 0), (0, chunk - T % chunk), (0, 0)))
        y, a_new = _rwkv_scan(pad(r), pad(lwd), pad(k_eff), pad(vc), pad(kk), pad(beta), _state_to_pairs(wkv0),
                              chunk, nb)
        y = y[:, :T]
    else:
        y, a_new = _rwkv_scan(r, lwd, k_eff, vc, kk, beta, _state_to_pairs(wkv0), chunk, nb)
    wkv_new = _pairs_to_state(a_new)
    if prompt:
        x = _merge(x, o_a, o_b, y, bonus, g, gates, lw, _tile(T, 512))
    else:
        fl = lambda a: a.reshape((1, B * T) + a.shape[2:])
        x = _merge(fl(x), fl(o_a), fl(o_b), fl(y), fl(bonus), fl(g), fl(gates), lw, B * T).reshape(B, T, D_MODEL)
    if prompt or T < FFN_CONV_W - 1:
        outs = _ffn(x, _pad_rows_front(ffn_hist, 8), lw, norm_f_g, final, _tile(T, 512))
        ffn_new = outs[1][:, 8 - (FFN_CONV_W - 1):]
    else:
        outs = _ffn_short(x, ffn_hist, lw, norm_f_g, final)
        ffn_new = outs[1]
    x = outs[0]
    xn = outs[2] if final else None
    state = (k.reshape(B, T, N_HEAD, D_HEAD), v.reshape(B, T, N_HEAD, D_HEAD), logf, conv_new, shift_new,
             wkv_new, ffn_new)
    return x, xn, state


def kernel(x_prompt, x_sample, cache_k, cache_v, cache_logf, state_conv, state_shift, state_wkv, state_ffn,
           page_table, norm1_g, w_in, b_f, b_gate, w_oa, conv_w, conv_b, conv_ln_g, conv_ln_b, w_ob,
           rwkv_mu, rwkv_w0, rwkv_w_lora_up, rwkv_a0, rwkv_a_lora_up, rwkv_g_lora_up, rwkv_k_k, rwkv_k_a,
           rwkv_r_k, rwkv_gn_g, rwkv_gn_b, w_oc, w_out, norm2_g, w_up_ffn, ffn_conv_w, w_down, norm_f_g):
    p = dict(norm1_g=norm1_g, w_in=w_in, b_f=b_f, b_gate=b_gate, w_oa=w_oa, conv_w=conv_w, conv_b=conv_b,
             conv_ln_g=conv_ln_g, conv_ln_b=conv_ln_b, w_ob=w_ob, rwkv_mu=rwkv_mu, rwkv_w0=rwkv_w0,
             rwkv_w_lora_up=rwkv_w_lora_up, rwkv_a0=rwkv_a0, rwkv_a_lora_up=rwkv_a_lora_up,
             rwkv_g_lora_up=rwkv_g_lora_up, rwkv_k_k=rwkv_k_k, rwkv_k_a=rwkv_k_a, rwkv_r_k=rwkv_r_k,
             rwkv_gn_g=rwkv_gn_g, rwkv_gn_b=rwkv_gn_b, w_oc=w_oc, w_out=w_out, norm2_g=norm2_g,
             w_up_ffn=w_up_ffn, ffn_conv_w=ffn_conv_w, w_down=w_down)
    depth = w_in.shape[0]
    nf = norm_f_g.reshape(1, D_MODEL)
    weights = [_layer_weights(l, p) for l in range(depth)]

    bp, tp, _ = x_prompt.shape
    bs, ts, _ = x_sample.shape
    page = cache_k.shape[2]

    yp, p_states = x_prompt, []
    for l in range(depth):
        attend = lambda qc, kc, vt: _fox_prompt(qc, kc, vt, _tile(tp, 512), _tile(tp, 1024))
        yp, ypn, st = _layer(yp, weights[l], attend,
                             jnp.zeros((bp, CONV_W - 1, W_HEADS), F32), jnp.zeros((bp, 1, C_SHIFT), F32),
                             jnp.zeros((bp, N_HEAD, D_HEAD, D_HEAD), F32), jnp.zeros((bp, FFN_CONV_W - 1, D_FF), F32),
                             nf, l == depth - 1, True)
        p_states.append(st)

    ys, s_states = x_sample, []
    n_pages = page_table.shape[1]
    n_group = 16 if n_pages % 16 == 0 else 1
    n_pool = cache_k.shape[1]
    kt_pool = jnp.transpose(cache_k, (0, 1, 3, 4, 2)).reshape(depth, n_pool, W_HEADS, page)
    vt_pool = jnp.transpose(cache_v, (0, 1, 3, 4, 2)).reshape(depth, n_pool, W_HEADS, page)
    lf_pool_t = jnp.swapaxes(cache_logf, 2, 3)
    for l in range(depth):

        def attend(q, k, v, logf, l=l):
            padk = lambda a: jnp.pad(a, ((0, 0), (0, page - ts), (0, 0)))
            lf_t = jnp.pad(jnp.swapaxes(logf, 1, 2), ((0, 0), (0, 0), (0, page - ts)))
            return _fox_sample(q, padk(k), padk(v), lf_t, l, kt_pool, vt_pool, lf_pool_t, page_table, n_group)

        ys, ysn, st = _layer(ys, weights[l], attend, state_conv[l], state_shift[l], state_wkv[l], state_ffn[l],
                             nf, l == depth - 1, False)
        s_states.append(st)

    stk = lambda states, i: jnp.stack([s[i] for s in states], axis=0)
    return (ypn, ysn,
            stk(p_states, 0), stk(p_states, 1), stk(p_states, 2), stk(p_states, 3),
            stk(p_states, 4), stk(p_states, 5), stk(p_states, 6),
            stk(s_states, 0), stk(s_states, 1), stk(s_states, 2), stk(s_states, 3),
            stk(s_states, 4), stk(s_states, 5), stk(s_states, 6))
```

```python
import functools

import numpy as np
import jax
import jax.numpy as jnp
from jax import lax
from jax.experimental import pallas as pl
from jax.experimental.pallas import tpu as pltpu

F32 = jnp.float32
BF16 = jnp.bfloat16

D_MODEL = 1024
N_HEAD = 8
D_HEAD = 64
W_HEADS = N_HEAD * D_HEAD
LANES = 128
N_PAIR = W_HEADS // LANES
C_SHIFT = 3 * W_HEADS + 64 + 64 + 128
D_FF = 2816
CONV_W = 31
FFN_CONV_W = 3
RMS_EPS = 1e-6
LN_EPS = 1e-5
GN_EPS = 64e-5
L2_EPS = 1e-12
NEG_BIG = -1e30
LOG2E = float(np.log2(np.e))
VMEM_LIMIT = 56 * 2**20


def _params(sem):
    return pltpu.CompilerParams(dimension_semantics=sem, vmem_limit_bytes=VMEM_LIMIT)


def _const_spec(shape):
    nd = len(shape)
    return pl.BlockSpec(shape, lambda *_: (0,) * nd, pipeline_mode=pl.Buffered(1))


def _dot(a, b):
    return jnp.dot(a.astype(BF16), b.astype(BF16), preferred_element_type=F32)


def _dot_nt(a, b):
    return lax.dot_general(a.astype(BF16), b.astype(BF16), (((1,), (1,)), ((), ())),
                           preferred_element_type=F32)


def _split3(x):
    hi = x.astype(BF16).astype(F32)
    r1 = x - hi
    mid = r1.astype(BF16).astype(F32)
    lo = (r1 - mid).astype(BF16).astype(F32)
    return hi, mid, lo


def _tri_dot(tri, x):
    pieces = jnp.concatenate(_split3(x), axis=1).astype(BF16)
    r = jnp.dot(tri, pieces, preferred_element_type=F32)
    return r[:, :LANES] + r[:, LANES:2 * LANES] + r[:, 2 * LANES:]


def _dot_tri(x, tri):
    pieces = jnp.concatenate(_split3(x), axis=0).astype(BF16)
    r = jnp.dot(pieces, tri, preferred_element_type=F32)
    n = x.shape[0]
    return r[:n] + r[n:2 * n] + r[2 * n:]


def _log_sigmoid(z):
    return jnp.minimum(z, 0.0) - jnp.log1p(jnp.exp(-jnp.abs(z)))


def _rmsnorm(x, g):
    return x * lax.rsqrt(jnp.mean(x * x, axis=-1, keepdims=True) + RMS_EPS) * g


def _inproj_body(prompt, x_ref, g_ref, wqkv_ref, wf_ref, bf_ref, wglu_ref, wr_ref, wg_ref, bg_ref, *rest):
    if prompt:
        (eplace_ref, qc_ref, kc_ref, k_ref, v_ref, vb_ref, lf_ref, u_ref, p_ref, gate_ref, fcarry) = rest
    else:
        (q_ref, k_ref, v_ref, lf_ref, u_ref, p_ref, gate_ref) = rest
    tm = x_ref.shape[1]
    hn = _rmsnorm(x_ref[0], g_ref[...]).astype(BF16)

    qkv = jnp.dot(hn, wqkv_ref[...], preferred_element_type=F32)
    q = qkv[:, :W_HEADS] * (D_HEAD ** -0.5 * (LOG2E if prompt else 1.0))
    k = qkv[:, W_HEADS:2 * W_HEADS]
    v = qkv[:, 2 * W_HEADS:]
    k_ref[0] = k
    v_ref[0] = v

    logf = _log_sigmoid(jnp.dot(hn, wf_ref[...], preferred_element_type=F32) + bf_ref[...])
    lf_ref[0] = logf[:, :N_HEAD]

    glu = jnp.dot(hn, wglu_ref[...], preferred_element_type=F32)
    u_ref[0] = glu[:, :W_HEADS] * jax.nn.sigmoid(glu[:, W_HEADS:])
    p_ref[0] = jnp.dot(hn, wr_ref[...], preferred_element_type=F32)
    gate_ref[0] = jax.nn.sigmoid(jnp.dot(hn, wg_ref[...], preferred_element_type=F32)
                                 + bg_ref[...]).astype(gate_ref.dtype)

    if not prompt:
        q_ref[0] = q.astype(q_ref.dtype)
        return

    v_t = v.T.astype(BF16)
    for pair in range(N_PAIR):
        vb_ref[0, pair] = v_t[pair * LANES:(pair + 1) * LANES, :]

    @pl.when(pl.program_id(1) == 0)
    def _():
        fcarry[...] = jnp.zeros_like(fcarry)

    lane = lax.broadcasted_iota(jnp.int32, (1, LANES), 1)
    lf = jnp.where(lane < N_HEAD, logf, 0.0)
    row = lax.broadcasted_iota(jnp.int32, (tm, tm), 0)
    col = lax.broadcasted_iota(jnp.int32, (tm, tm), 1)
    tri = jnp.where(col <= row, 1.0, 0.0).astype(BF16)
    fcum = _tri_dot(tri, lf) + fcarry[...]
    fcarry[...] = fcum[tm - 1:tm, :]
    aug = jnp.dot(jnp.concatenate(_split3(fcum * LOG2E), axis=1).astype(BF16), eplace_ref[...],
                  preferred_element_type=F32)
    slot = lane & 15
    augq = aug[:, :LANES] + jnp.where((slot >= 3) & (slot < 6), 1.0, 0.0)
    augk = aug[:, LANES:] + jnp.where(slot < 3, 1.0, 0.0)
    augk_b = augk.astype(BF16)
    for pair in range(N_PAIR):
        kc_ref[0, pair] = jnp.concatenate([k[:, pair * LANES:(pair + 1) * LANES].astype(BF16), augk_b], axis=1)
        q_pair = q[:, pair * LANES:(pair + 1) * LANES]
        for par in range(2):
            h = 2 * pair + par
            q_h = jnp.where((lane >= D_HEAD) if par else (lane < D_HEAD), q_pair, 0.0)
            a_h = jnp.where((lane >> 4) == h, augq, 0.0)
            qc_ref[0, h] = jnp.concatenate([q_h, a_h], axis=1).astype(BF16)


def _eplace():
    e = np.zeros((3 * LANES, 2 * LANES), np.float32)
    for piece in range(3):
        for h in range(N_HEAD):
            e[piece * LANES + h, 16 * h + piece] = 1.0
            e[piece * LANES + h, LANES + 16 * h + 3 + piece] = -1.0
    return jnp.asarray(e, BF16)


def _in_proj(x, lw, prompt, tm):
    B, T, _ = x.shape
    nt = T // tm
    row = lambda width: pl.BlockSpec((1, tm, width), lambda b, i: (b, i, 0))
    weights = [lw['norm1_g'], lw['w_qkv'], lw['w_f'], lw['b_f'], lw['w_glu'], lw['w_r'], lw['w_g'], lw['b_gate']]
    in_specs = [row(D_MODEL)] + [_const_spec(w.shape) for w in weights]
    common_shapes = [
        jax.ShapeDtypeStruct((B, T, W_HEADS), F32),
        jax.ShapeDtypeStruct((B, T, W_HEADS), F32),
    ]
    tail_shapes = [
        jax.ShapeDtypeStruct((B, T, N_HEAD), F32),
        jax.ShapeDtypeStruct((B, T, W_HEADS), F32),
        jax.ShapeDtypeStruct((B, T, C_SHIFT), F32),
        jax.ShapeDtypeStruct((B, T, 3 * D_MODEL), BF16),
    ]
    tail_specs = [row(N_HEAD), row(W_HEADS), row(C_SHIFT), row(3 * D_MODEL)]
    if prompt:
        args = [x] + weights + [_eplace()]
        in_specs = in_specs + [_const_spec((3 * LANES, 2 * LANES))]
        out_shape = [jax.ShapeDtypeStruct((B, N_HEAD, T, 2 * LANES), BF16),
                     jax.ShapeDtypeStruct((B, N_PAIR, T, 2 * LANES), BF16)] + common_shapes + \
                    [jax.ShapeDtypeStruct((B, N_PAIR, LANES, T), BF16)] + tail_shapes
        out_specs = [pl.BlockSpec((1, N_HEAD, tm, 2 * LANES), lambda b, i: (b, 0, i, 0)),
                     pl.BlockSpec((1, N_PAIR, tm, 2 * LANES), lambda b, i: (b, 0, i, 0)),
                     row(W_HEADS), row(W_HEADS),
                     pl.BlockSpec((1, N_PAIR, LANES, tm), lambda b, i: (b, 0, 0, i))] + tail_specs
        scratch = [pltpu.VMEM((1, LANES), F32)]
    else:
        args = [x] + weights
        out_shape = [jax.ShapeDtypeStruct((B, T, W_HEADS), BF16)] + common_shapes + tail_shapes
        out_specs = [row(W_HEADS), row(W_HEADS), row(W_HEADS)] + tail_specs
        scratch = []
    return pl.pallas_call(
        functools.partial(_inproj_body, prompt),
        grid=(B, nt), in_specs=in_specs, out_specs=out_specs, out_shape=out_shape,
        scratch_shapes=scratch, compiler_params=_params(("arbitrary", "arbitrary")),
        name="in_proj_prompt" if prompt else "in_proj_sample",
    )(*args)


ONES_ROWS = 16


def _fox_prompt_body(qc_ref, kc_ref, vt_ref, o_ref, v1t, *, tq, tk):
    i = pl.program_id(2)
    n_pp = kc_ref.shape[1]
    heads = range(2 * n_pp)
    n_kt = v1t.shape[1]

    @pl.when(i == 0)
    def _():
        for pp in range(n_pp):
            for j in range(n_kt):
                v1t[pp, j, :LANES, :] = vt_ref[0, pp, :, j * tk:(j + 1) * tk]
                v1t[pp, j, LANES:, :] = jnp.ones((ONES_ROWS, tk), BF16)

    qs = [qc_ref[0, h] for h in heads]

    def tile(j, carry, masked):
        start = pl.multiple_of(j * tk, tk)
        kcs = [kc_ref[0, pp, pl.ds(start, tk), :] for pp in range(n_pp)]
        vts = [v1t[pp, j] for pp in range(n_pp)]
        ss = [lax.dot_general(kcs[h // 2], qs[h], (((1,), (1,)), ((), ())), preferred_element_type=F32)
              for h in heads]
        if masked:
            key = start + lax.broadcasted_iota(jnp.int32, (tk, tq), 0)
            qpos = i * tq + lax.broadcasted_iota(jnp.int32, (tk, tq), 1)
            ss = [jnp.where(key <= qpos, s, NEG_BIG) for s in ss]
        out = []
        for pp in range(n_pp):
            pair = (2 * pp, 2 * pp + 1)
            m_new = [jnp.maximum(carry[h][0], jnp.max(ss[h], axis=0, keepdims=True)) for h in pair]
            ps = [jnp.exp2(ss[h] - m).astype(BF16) for h, m in zip(pair, m_new)]
            accs = [jnp.exp2(carry[h][0] - m) * carry[h][1] + jnp.dot(vts[pp], p, preferred_element_type=F32)
                    for h, m, p in zip(pair, m_new, ps)]
            out.extend(zip(m_new, accs))
        return tuple(out)

    init = tuple((jnp.full((1, tq), NEG_BIG, F32), jnp.zeros((LANES + ONES_ROWS, tq), F32)) for _ in heads)
    n_full = (i * tq) // tk
    carry = lax.fori_loop(0, n_full, lambda j, c: tile(j, c, False), init)
    final = tile(n_full, carry, True)
    chan = lax.broadcasted_iota(jnp.int32, (LANES, 1), 0)
    for pp in range(n_pp):
        a0, a1 = final[2 * pp][1], final[2 * pp + 1][1]
        o_t = jnp.where(chan < D_HEAD, a0[:LANES] / a0[LANES:LANES + 1], a1[:LANES] / a1[LANES:LANES + 1])
        o_ref[0, :, pp * LANES:(pp + 1) * LANES] = o_t.T.astype(o_ref.dtype)


def _fox_prompt(qc, kc, vt, tq, tk, n_pp=2):
    B, _, T, _ = qc.shape
    return pl.pallas_call(
        functools.partial(_fox_prompt_body, tq=tq, tk=tk),
        grid=(B, N_PAIR // n_pp, T // tq),
        in_specs=[pl.BlockSpec((1, 2 * n_pp, tq, 2 * LANES), lambda b, p, i: (b, p, i, 0)),
                  pl.BlockSpec((1, n_pp, T, 2 * LANES), lambda b, p, i: (b, p, 0, 0)),
                  pl.BlockSpec((1, n_pp, LANES, T), lambda b, p, i: (b, p, 0, 0))],
        out_specs=pl.BlockSpec((1, tq, n_pp * LANES), lambda b, p, i: (b, i, p)),
        out_shape=jax.ShapeDtypeStruct((B, T, W_HEADS), BF16),
        scratch_shapes=[pltpu.VMEM((n_pp, T // tk, LANES + ONES_ROWS, tk), BF16)],
        compiler_params=_params(("arbitrary", "arbitrary", "arbitrary")),
        name="fox_prompt",
    )(qc, kc, vt)


def _fox_sample_body(n_group, t_new, pt_ref, q_ref, kn_ref, vn_ref, lfn_ref, *rest):
    k_refs = rest[:n_group]
    v_refs = rest[n_group:2 * n_group]
    lf_refs = rest[2 * n_group:3 * n_group]
    o_ref, qbd, m_s, l_s, acc_s, tail_s, newsum_s = rest[3 * n_group:]
    j = pl.program_id(1)
    nrow = t_new * N_HEAD
    page = lf_refs[0].shape[1]
    lane_w = lax.broadcasted_iota(jnp.int32, (N_HEAD, W_HEADS), 1)
    head_w = lax.broadcasted_iota(jnp.int32, (N_HEAD, W_HEADS), 0)
    bd_mask = (lane_w >> 6) == head_w
    ri = lax.broadcasted_iota(jnp.int32, (page, page), 0)
    ci = lax.broadcasted_iota(jnp.int32, (page, page), 1)

    def tile4(x):
        return jnp.concatenate([x] * t_new, axis=0)

    def online(s, pv_fn):
        m_new = jnp.maximum(m_s[...], jnp.max(s, axis=-1, keepdims=True))
        alpha = jnp.exp(m_s[...] - m_new)
        p = jnp.exp(s - m_new)
        l_s[...] = alpha * l_s[...] + jnp.sum(p, axis=-1, keepdims=True)
        acc_s[...] = alpha * acc_s[...] + pv_fn(p.astype(BF16))
        m_s[...] = m_new

    @pl.when(j == 0)
    def _():
        q = q_ref[0].astype(F32)
        qbd[...] = jnp.concatenate(
            [jnp.where(bd_mask, jnp.broadcast_to(q[t:t + 1, :], (N_HEAD, W_HEADS)), 0.0)
             for t in range(t_new)], axis=0).astype(BF16)
        m_s[...] = jnp.full_like(m_s, NEG_BIG)
        l_s[...] = jnp.zeros_like(l_s)
        acc_s[...] = jnp.zeros_like(acc_s)
        tail_s[...] = jnp.zeros_like(tail_s)
        tri_incl = jnp.where(ri <= ci, 1.0, 0.0).astype(BF16)
        cum = _dot_tri(lfn_ref[0], tri_incl)
        newsum = jnp.concatenate([cum[:, t:t + 1] for t in range(t_new)], axis=0)
        newsum_s[...] = newsum
        s = _dot_nt(qbd[...], kn_ref[0])
        rr = lax.broadcasted_iota(jnp.int32, (nrow, page), 0) >> 3
        cc = lax.broadcasted_iota(jnp.int32, (nrow, page), 1)
        s = jnp.where(cc <= rr, s + newsum - tile4(cum), NEG_BIG)
        online(s, lambda p: _dot(p, vn_ref[0]))

    tri_after = jnp.where(ri > ci, 1.0, 0.0).astype(BF16)
    lfs = [lf_refs[g][...] for g in range(n_group)]
    pieces = jnp.concatenate([jnp.concatenate(_split3(lf), axis=0) for lf in lfs], axis=0).astype(BF16)
    suffix = jnp.dot(pieces, tri_after, preferred_element_type=F32)
    qb = qbd[...]
    tail = tail_s[...]
    parts = []
    for g in range(n_group):
        r0 = 3 * N_HEAD * g
        bias8 = (suffix[r0:r0 + N_HEAD] + suffix[r0 + N_HEAD:r0 + 2 * N_HEAD]
                 + suffix[r0 + 2 * N_HEAD:r0 + 3 * N_HEAD] + tail)
        parts.append(_dot(qb, k_refs[g][...]) + tile4(bias8))
        tail = tail + jnp.sum(lfs[g], axis=-1, keepdims=True)
    tail_s[...] = tail
    s_all = jnp.concatenate(parts, axis=1) + newsum_s[...]

    def pv_pages(p):
        acc = _dot_nt(p[:, :page], v_refs[0][...])
        for g in range(1, n_group):
            acc = acc + _dot_nt(p[:, g * page:(g + 1) * page], v_refs[g][...])
        return acc

    online(s_all, pv_pages)

    @pl.when(j == pl.num_programs(1) - 1)
    def _():
        o = acc_s[...] / l_s[...]
        for t in range(t_new):
            o_ref[0, t:t + 1, :] = jnp.sum(jnp.where(bd_mask, o[t * N_HEAD:(t + 1) * N_HEAD, :], 0.0), axis=0,
                                           keepdims=True).astype(o_ref.dtype)


def _fox_sample(q, k_new, v_new, lf_new_t, layer, kt_pool, vt_pool, lf_pool_t, page_table, n_group):
    B, t_new, _ = q.shape
    page = kt_pool.shape[3]
    n_pages = page_table.shape[1]
    steps = n_pages // n_group

    def paged(width_shape):
        specs = []
        for g in range(n_group):
            specs.append(pl.BlockSpec((None, None) + width_shape,
                                      lambda b, j, pt, g=g: (layer, pt[b, n_pages - 1 - (j * n_group + g)], 0, 0)))
        return specs

    per_b = lambda shape: pl.BlockSpec((1,) + shape, lambda b, j, pt: (b, 0, 0))
    nrow = t_new * N_HEAD
    grid_spec = pltpu.PrefetchScalarGridSpec(
        num_scalar_prefetch=1, grid=(B, steps),
        in_specs=[per_b((t_new, W_HEADS)), per_b((page, W_HEADS)), per_b((page, W_HEADS)), per_b((N_HEAD, page))]
        + paged((W_HEADS, page)) + paged((W_HEADS, page)) + paged((N_HEAD, page)),
        out_specs=per_b((t_new, W_HEADS)),
        scratch_shapes=[pltpu.VMEM((nrow, W_HEADS), BF16), pltpu.VMEM((nrow, 1), F32), pltpu.VMEM((nrow, 1), F32),
                        pltpu.VMEM((nrow, W_HEADS), F32), pltpu.VMEM((N_HEAD, 1), F32), pltpu.VMEM((nrow, 1), F32)])
    return pl.pallas_call(
        functools.partial(_fox_sample_body, n_group, t_new),
        grid_spec=grid_spec,
        out_shape=jax.ShapeDtypeStruct((B, t_new, W_HEADS), F32),
        compiler_params=_params(("arbitrary", "arbitrary")),
        name="fox_sample",
    )(page_table, q, k_new, v_new, lf_new_t, *([kt_pool] * n_group), *([vt_pool] * n_group),
      *([lf_pool_t] * n_group))


HIST_ROWS = 32


SUBLANES = 8
CONV_ROWS = 32


def _conv_body(u_ref, hist_ref, w_ref, b_ref, g_ref, beta_ref, o_ref, ext, shifted):
    tt = u_ref.shape[1]

    @pl.when(pl.program_id(1) == 0)
    def _():
        ext[0:HIST_ROWS, :] = hist_ref[0]

    ext[HIST_ROWS:HIST_ROWS + tt, :] = u_ref[0]
    span = tt + HIST_ROWS - SUBLANES
    for s in range(1, SUBLANES):
        shifted[s - 1, 0:span, :] = ext[s:s + span, :]
    first = HIST_ROWS - (CONV_W - 1)
    rows = min(tt, CONV_ROWS)
    for c0 in range(0, tt, rows):
        acc = jnp.zeros((rows, W_HEADS), F32)
        for tap in range(CONV_W):
            s = (first + tap) % SUBLANES
            base = first + tap - s + c0
            win = ext[base:base + rows, :] if s == 0 else shifted[s - 1, base:base + rows, :]
            acc = acc + w_ref[tap:tap + 1, :] * win
        c = acc + b_ref[...]
        mean = jnp.mean(c, axis=-1, keepdims=True)
        xc = c - mean
        var = jnp.mean(xc * xc, axis=-1, keepdims=True)
        y = xc * lax.rsqrt(var + LN_EPS) * g_ref[...] + beta_ref[...]
        o_ref[0, c0:c0 + rows, :] = (y * jax.nn.sigmoid(y)).astype(o_ref.dtype)
    ext[0:HIST_ROWS, :] = ext[tt:tt + HIST_ROWS, :]


def _conv(u, hist, lw, tt):
    B, T, _ = u.shape
    return pl.pallas_call(
        _conv_body, grid=(B, T // tt),
        in_specs=[pl.BlockSpec((1, tt, W_HEADS), lambda b, i: (b, i, 0)),
                  pl.BlockSpec((1, HIST_ROWS, W_HEADS), lambda b, i: (b, 0, 0)),
                  _const_spec((CONV_W, W_HEADS)), _const_spec((1, W_HEADS)), _const_spec((1, W_HEADS)),
                  _const_spec((1, W_HEADS))],
        out_specs=pl.BlockSpec((1, tt, W_HEADS), lambda b, i: (b, i, 0)),
        out_shape=jax.ShapeDtypeStruct((B, T, W_HEADS), BF16),
        scratch_shapes=[pltpu.VMEM((HIST_ROWS + tt, W_HEADS), F32),
                        pltpu.VMEM((SUBLANES - 1, HIST_ROWS + tt - SUBLANES, W_HEADS), F32)],
        compiler_params=_params(("arbitrary", "arbitrary")),
        name="conv",
    )(u, hist, lw['conv_w'], lw['conv_b'], lw['conv_ln_g'], lw['conv_ln_b'])


def _rwkv_prep_body(p_ref, hist_ref, mu_ref, w0_ref, a0_ref, kk_ref, ka_ref, rk_ref, wwa_ref, wg_ref, bd_ref,
                    r_o, lw_o, k_o, v_o, kk_o, b_o, g_o, bonus_o, ext):
    tt = p_ref.shape[1]

    @pl.when(pl.program_id(1) == 0)
    def _():
        ext[0:8, :] = hist_ref[0]

    p = p_ref[0]
    ext[8:8 + tt, :] = p
    prev = ext[7:7 + tt, :]
    ps = p + (prev - p) * mu_ref[...]
    ext[0:8, :] = ext[tt:tt + 8, :]

    r = ps[:, :W_HEADS]
    kc = ps[:, W_HEADS:2 * W_HEADS]
    vc = ps[:, 2 * W_HEADS:3 * W_HEADS]
    wa = ps[:, 3 * W_HEADS:3 * W_HEADS + LANES]
    gd = ps[:, 3 * W_HEADS + LANES:]
    lane = lax.broadcasted_iota(jnp.int32, (1, LANES), 1)
    wa = jnp.where(lane < 64, jnp.tanh(wa), wa)
    lora = _dot(wa, wwa_ref[...])
    lw_o[0] = -float(np.exp(-0.5)) * jax.nn.sigmoid(w0_ref[...] + lora[:, :W_HEADS])
    a = jax.nn.sigmoid(a0_ref[...] + lora[:, W_HEADS:])
    g_o[0] = _dot(jax.nn.sigmoid(gd), wg_ref[...])
    bd = bd_ref[...]
    kk = kc * kk_ref[...]
    sq = kk * kk
    sq_hi = sq.astype(BF16)
    sq_lo = (sq - sq_hi.astype(F32)).astype(BF16)
    ss = jnp.dot(sq_hi, bd, preferred_element_type=F32) + jnp.dot(sq_lo, bd, preferred_element_type=F32)
    kk = kk / jnp.maximum(jnp.sqrt(ss), L2_EPS)
    k_eff = kc * (1.0 + (a - 1.0) * ka_ref[...])
    r_o[0] = r
    k_o[0] = k_eff
    v_o[0] = vc
    kk_o[0] = kk
    b_o[0] = kk * a
    bonus_o[0] = _dot(r * k_eff * rk_ref[...], bd) * vc


def _rwkv_prep(p, hist, lw, tt):
    B, T, _ = p.shape
    row = lambda width: pl.BlockSpec((1, tt, width), lambda b, i: (b, i, 0))
    consts = [lw['rwkv_mu'], lw['rwkv_w0'], lw['rwkv_a0'], lw['rwkv_k_k'], lw['rwkv_k_a'], lw['rwkv_r_k'],
              lw['w_wa'], lw['rwkv_g_lora_up'], lw['bd_ones']]
    return pl.pallas_call(
        _rwkv_prep_body, grid=(B, T // tt),
        in_specs=[row(C_SHIFT), pl.BlockSpec((1, 8, C_SHIFT), lambda b, i: (b, 0, 0))]
        + [_const_spec(c.shape) for c in consts],
        out_specs=[row(W_HEADS)] * 8,
        out_shape=[jax.ShapeDtypeStruct((B, T, W_HEADS), F32)] * 8,
        scratch_shapes=[pltpu.VMEM((8 + tt, C_SHIFT), F32)],
        compiler_params=_params(("arbitrary", "arbitrary")),
        name="rwkv_prep",
    )(p, hist, *consts)


def _rwkv_scan_body(r_ref, lw_ref, k_ref, v_ref, kk_ref, b_ref, a0_ref, y_ref, a_out_ref, a_s, *, chunk, nb):
    c = chunk
    n = 2 * c

    @pl.when(pl.program_id(1) == 0)
    def _():
        a_s[...] = a0_ref[...]

    ri = lax.broadcasted_iota(jnp.int32, (n, n), 0)
    ci = lax.broadcasted_iota(jnp.int32, (n, n), 1)
    strict = ri > ci
    incl = ri >= ci
    eye_f = jnp.where(ri == ci, 1.0, 0.0)
    tr = lax.broadcasted_iota(jnp.int32, (c, c), 0)
    tc = lax.broadcasted_iota(jnp.int32, (c, c), 1)
    tri_incl = jnp.where(tc <= tr, 1.0, 0.0).astype(BF16)
    lane = lax.broadcasted_iota(jnp.int32, (1, LANES), 1)
    m_even = lane < D_HEAD

    def stack(x):
        return jnp.concatenate([jnp.where(m_even, x, 0.0), jnp.where(m_even, 0.0, x)], axis=0)

    units = [(bi, pair) for bi in range(nb) for pair in range(N_PAIR)]

    def load(ref):
        return [ref[bi, :, pair * LANES:(pair + 1) * LANES] for bi, pair in units]

    lw = load(lw_ref)
    cum = [_tri_dot(tri_incl, x) for x in lw]
    cum_end = [x[c - 1:c, :] for x in cum]
    e_neg = [jnp.exp(-x) for x in cum]
    e_end = [jnp.exp(ce - x) for ce, x in zip(cum_end, cum)]
    r, k, v, kk, b = load(r_ref), load(k_ref), load(v_ref), load(kk_ref), load(b_ref)
    rt = [stack(x * jnp.exp(cu)).astype(BF16) for x, cu in zip(r, cum)]
    ka = [stack(x * jnp.exp(cu - l)).astype(BF16) for x, cu, l in zip(kk, cum, lw)]
    kt = [stack(x * e).astype(BF16) for x, e in zip(k, e_neg)]
    bt = [stack(x * e).astype(BF16) for x, e in zip(b, e_neg)]
    kh_t = [stack(x * e).T.astype(BF16) for x, e in zip(k, e_end)]
    bh_t = [stack(x * e).T.astype(BF16) for x, e in zip(b, e_end)]
    vs = [stack(x).astype(BF16) for x in v]

    big = [_dot_nt(jnp.concatenate([a, q], axis=0), jnp.concatenate([x, y], axis=0))
           for a, q, x, y in zip(ka, rt, kt, bt)]
    m_mat = [jnp.where(strict, x[:n, :n], 0.0).astype(BF16) for x in big]
    l_mat = [jnp.where(strict, x[:n, n:], 0.0) for x in big]
    n1 = [jnp.where(incl, x[n:, :n], 0.0).astype(BF16) for x in big]
    n2 = [jnp.where(incl, x[n:, n:], 0.0).astype(BF16) for x in big]

    inv = [eye_f - jnp.where((ri >> 1) == (ci >> 1), x, 0.0) for x in l_mat]
    size = 2
    shift = 1
    while size < c:
        blk = ((ri >> (shift + 1)) == (ci >> (shift + 1))) & (((ri >> shift) & 1) == 1) & (((ci >> shift) & 1) == 0)
        inv_b = [x.astype(BF16) for x in inv]
        half = [_dot(x, jnp.where(blk, l, 0.0)) for x, l in zip(inv_b, l_mat)]
        inv = [x - _dot(h, xb) for x, h, xb in zip(inv, half, inv_b)]
        size *= 2
        shift += 1
    inv_b = [x.astype(BF16) for x in inv]

    mv = [_dot(m, x).astype(BF16) for m, x in zip(m_mat, vs)]
    wu = [_dot(i, jnp.concatenate([a, x], axis=1)).astype(BF16) for i, a, x in zip(inv_b, ka, mv)]
    nw = [_dot(x, w) for x, w in zip(n2, wu)]
    rq = [q.astype(F32) - x[:, :LANES] for q, x in zip(rt, nw)]
    y0 = [_dot(m, x) - w[:, LANES:] for m, x, w in zip(n1, vs, nw)]
    bw = [_dot(x, w) for x, w in zip(bh_t, wu)]
    eye_l = lax.broadcasted_iota(jnp.int32, (LANES, LANES), 0) == lax.broadcasted_iota(jnp.int32, (LANES, LANES), 1)
    g_mat = [jnp.where(eye_l, jnp.broadcast_to(jnp.exp(ce), (LANES, LANES)), 0.0) - x[:, :LANES]
             for ce, x in zip(cum_end, bw)]
    h_mat = [_dot(x, y) - w[:, LANES:] for x, y, w in zip(kh_t, vs, bw)]
    a_prev = [a_s[bi, pair].astype(BF16) for bi, pair in units]
    ys = [_dot(q, a) + y for q, a, y in zip(rq, a_prev, y0)]
    a_new = [_dot(g, a) + h for g, a, h in zip(g_mat, a_prev, h_mat)]
    for (bi, pair), y, a in zip(units, ys, a_new):
        y_ref[bi, :, pair * LANES:(pair + 1) * LANES] = y[:c] + y[c:]
        a_s[bi, pair] = a

    @pl.when(pl.program_id(1) == pl.num_programs(1) - 1)
    def _():
        a_out_ref[...] = a_s[...]


def _rwkv_scan(r, lwd, k, v, kk, b, a0, chunk, nb):
    B, T, _ = r.shape
    row = pl.BlockSpec((nb, chunk, W_HEADS), lambda bb, i: (bb, i, 0))
    st = pl.BlockSpec((nb, N_PAIR, LANES, LANES), lambda bb, i: (bb, 0, 0, 0))
    return pl.pallas_call(
        functools.partial(_rwkv_scan_body, chunk=chunk, nb=nb), grid=(B // nb, T // chunk),
        in_specs=[row] * 6 + [st], out_specs=[row, st],
        out_shape=[jax.ShapeDtypeStruct((B, T, W_HEADS), F32),
                   jax.ShapeDtypeStruct((B, N_PAIR, LANES, LANES), F32)],
        scratch_shapes=[pltpu.VMEM((nb, N_PAIR, LANES, LANES), F32)],
        compiler_params=_params(("arbitrary", "arbitrary")),
        name="rwkv_scan",
    )(r, lwd, k, v, kk, b, a0)


def _merge_body(x_ref, oa_ref, ob_ref, y_ref, bonus_ref, g_ref, gate_ref, woa_ref, wob_ref, woc_ref, wout_ref,
                gng_ref, gnb_ref, bd_ref, o_ref):
    bd = bd_ref[...]
    y = y_ref[0]
    mean = _dot(y, bd) * (1.0 / D_HEAD)
    xc = y - mean
    var = _dot(xc * xc, bd) * (1.0 / D_HEAD)
    yn = xc * lax.rsqrt(var + GN_EPS) * gng_ref[...] + gnb_ref[...]
    oc = (yn + bonus_ref[0]) * g_ref[0]
    gates = gate_ref[0].astype(F32)
    merged = (gates[:, :D_MODEL] * _dot(oa_ref[0], woa_ref[...])
              + gates[:, D_MODEL:2 * D_MODEL] * _dot(ob_ref[0], wob_ref[...])
              + gates[:, 2 * D_MODEL:] * _dot(oc, woc_ref[...]))
    o_ref[0] = x_ref[0] + _dot(merged, wout_ref[...])


def _merge(x, oa, ob, y, bonus, g, gates, lw, tm):
    B, T, _ = x.shape
    row = lambda width: pl.BlockSpec((1, tm, width), lambda b, i: (b, i, 0))
    consts = [lw['w_oa'], lw['w_ob'], lw['w_oc'], lw['w_out'], lw['rwkv_gn_g'], lw['rwkv_gn_b'], lw['bd_ones']]
    return pl.pallas_call(
        _merge_body, grid=(B, T // tm),
        in_specs=[row(D_MODEL), row(W_HEADS), row(W_HEADS), row(W_HEADS), row(W_HEADS), row(W_HEADS),
                  row(3 * D_MODEL)] + [_const_spec(c.shape) for c in consts],
        out_specs=row(D_MODEL), out_shape=jax.ShapeDtypeStruct((B, T, D_MODEL), F32),
        compiler_params=_params(("arbitrary", "arbitrary")),
        name="merge",
    )(x, oa, ob, y, bonus, g, gates, *consts)


def _gelu_tanh(x):
    return 0.5 * x * (1.0 + jnp.tanh(float(np.sqrt(2.0 / np.pi)) * (x + 0.044715 * (x * x * x))))


def _ffn_body(final, x_ref, hist_ref, g2_ref, wa_ref, wb_ref, cw_ref, wd_ref, gf_ref, *rest):
    if final:
        o_ref, new_ref, yn_ref, ext = rest
    else:
        o_ref, new_ref, ext = rest
    tm = x_ref.shape[1]

    @pl.when(pl.program_id(1) == 0)
    def _():
        ext[0:8, :] = hist_ref[0]

    x = x_ref[0]
    hn = _rmsnorm(x, g2_ref[...]).astype(BF16)
    ext[8:8 + tm, :] = jnp.dot(hn, wa_ref[...], preferred_element_type=F32)
    ub = jnp.dot(hn, wb_ref[...], preferred_element_type=F32)
    c2 = (cw_ref[0:1, :] * ext[6:6 + tm, :] + cw_ref[1:2, :] * ext[7:7 + tm, :]
          + cw_ref[2:3, :] * ext[8:8 + tm, :])
    out = x + _dot(_gelu_tanh(c2) * ub, wd_ref[...])
    o_ref[0] = out
    last = ext[tm:tm + 8, :]
    new_ref[0] = last
    ext[0:8, :] = last
    if final:
        yn_ref[0] = _rmsnorm(out, gf_ref[...])


def _ffn_short_body(final, t_seq, x_ref, prev_ref, g2_ref, wa_ref, wb_ref, cw_ref, wd_ref, gf_ref, *rest):
    if final:
        o_ref, ua_ref, yn_ref, ext = rest
    else:
        o_ref, ua_ref, ext = rest
    tm = x_ref.shape[1]
    x = x_ref[0]
    hn = _rmsnorm(x, g2_ref[...]).astype(BF16)
    ua = jnp.dot(hn, wa_ref[...], preferred_element_type=F32)
    ub = jnp.dot(hn, wb_ref[...], preferred_element_type=F32)
    ua_ref[0] = ua
    ext[0:8, :] = jnp.zeros((8, D_FF), F32)
    ext[8:8 + tm, :] = ua
    t = lax.rem(lax.broadcasted_iota(jnp.int32, (tm, 1), 0), t_seq)
    back1 = jnp.where(t >= 1, ext[7:7 + tm, :], prev_ref[0])
    back2 = jnp.where(t >= 2, ext[6:6 + tm, :], prev_ref[1])
    c2 = cw_ref[0:1, :] * back2 + cw_ref[1:2, :] * back1 + cw_ref[2:3, :] * ua
    out = x + _dot(_gelu_tanh(c2) * ub, wd_ref[...])
    o_ref[0] = out
    if final:
        yn_ref[0] = _rmsnorm(out, gf_ref[...])


def _ffn_short(x, hist, lw, norm_f_g, final):
    B, T, _ = x.shape
    M = B * T
    zeros = jnp.zeros((B, T, D_FF), F32)
    prev1 = zeros.at[:, 0].set(hist[:, 1])
    prev2 = zeros.at[:, 0].set(hist[:, 0]).at[:, 1].set(hist[:, 1])
    prev = jnp.stack([prev1.reshape(M, D_FF), prev2.reshape(M, D_FF)])
    row = _const_spec((1, M, D_MODEL))
    wide = _const_spec((1, M, D_FF))
    consts = [lw['norm2_g'], lw['w_up_a'], lw['w_up_b'], lw['ffn_conv_w'], lw['w_down'], norm_f_g]
    out_specs = [row, wide]
    out_shape = [jax.ShapeDtypeStruct((1, M, D_MODEL), F32), jax.ShapeDtypeStruct((1, M, D_FF), F32)]
    if final:
        out_specs.append(row)
        out_shape.append(jax.ShapeDtypeStruct((1, M, D_MODEL), F32))
    outs = pl.pallas_call(
        functools.partial(_ffn_short_body, final, T), grid=(1,),
        in_specs=[row, _const_spec((2, M, D_FF))] + [_const_spec(c.shape) for c in consts],
        out_specs=out_specs, out_shape=out_shape,
        scratch_shapes=[pltpu.VMEM((8 + M, D_FF), F32)],
        compiler_params=_params(("arbitrary",)),
        name="ffn_short",
    )(x.reshape(1, M, D_MODEL), prev, *consts)
    res = [outs[0].reshape(B, T, D_MODEL), outs[1].reshape(B, T, D_FF)[:, T - (FFN_CONV_W - 1):]]
    if final:
        res.append(outs[2].reshape(B, T, D_MODEL))
    return res


def _ffn(x, hist, lw, norm_f_g, final, tm):
    B, T, _ = x.shape
    row = pl.BlockSpec((1, tm, D_MODEL), lambda b, i: (b, i, 0))
    hist_spec = pl.BlockSpec((1, 8, D_FF), lambda b, i: (b, 0, 0))
    consts = [lw['norm2_g'], lw['w_up_a'], lw['w_up_b'], lw['ffn_conv_w'], lw['w_down'], norm_f_g]
    out_specs = [row, hist_spec]
    out_shape = [jax.ShapeDtypeStruct((B, T, D_MODEL), F32), jax.ShapeDtypeStruct((B, 8, D_FF), F32)]
    if final:
        out_specs.append(row)
        out_shape.append(jax.ShapeDtypeStruct((B, T, D_MODEL), F32))
    return pl.pallas_call(
        functools.partial(_ffn_body, final), grid=(B, T // tm),
        in_specs=[row, hist_spec] + [_const_spec(c.shape) for c in consts],
        out_specs=out_specs, out_shape=out_shape,
        scratch_shapes=[pltpu.VMEM((8 + tm, D_FF), F32)],
        compiler_params=_params(("arbitrary", "arbitrary")),
        name="ffn",
    )(x, hist, *consts)


def _layer_weights(l, p):
    w_in = p['w_in'][l]
    o_f = 3 * W_HEADS
    o_glu = o_f + N_HEAD
    o_r = o_glu + 2 * W_HEADS
    o_g = o_r + C_SHIFT
    row = lambda a: a[l].reshape(1, -1)
    wwa = jnp.zeros((LANES, 2 * W_HEADS), F32)
    wwa = wwa.at[:64, :W_HEADS].set(p['rwkv_w_lora_up'][l]).at[64:, W_HEADS:].set(p['rwkv_a_lora_up'][l])
    head = np.arange(W_HEADS) // D_HEAD
    return {
        'norm1_g': row(p['norm1_g']),
        'w_qkv': w_in[:, :o_f].astype(BF16),
        'w_f': jnp.pad(w_in[:, o_f:o_glu], ((0, 0), (0, LANES - N_HEAD))).astype(BF16),
        'b_f': jnp.pad(p['b_f'][l], (0, LANES - N_HEAD)).reshape(1, LANES),
        'w_glu': w_in[:, o_glu:o_r].astype(BF16),
        'w_r': w_in[:, o_r:o_g].astype(BF16),
        'w_g': w_in[:, o_g:].astype(BF16),
        'b_gate': row(p['b_gate']),
        'conv_w': p['conv_w'][l], 'conv_b': row(p['conv_b']), 'conv_ln_g': row(p['conv_ln_g']),
        'conv_ln_b': row(p['conv_ln_b']),
        'rwkv_mu': row(p['rwkv_mu']), 'rwkv_w0': row(p['rwkv_w0']), 'rwkv_a0': row(p['rwkv_a0']),
        'rwkv_k_k': row(p['rwkv_k_k']), 'rwkv_k_a': row(p['rwkv_k_a']), 'rwkv_r_k': row(p['rwkv_r_k']),
        'w_wa': wwa.astype(BF16), 'rwkv_g_lora_up': p['rwkv_g_lora_up'][l].astype(BF16),
        'bd_ones': jnp.asarray(head[:, None] == head[None, :], BF16),
        'rwkv_gn_g': row(p['rwkv_gn_g']), 'rwkv_gn_b': row(p['rwkv_gn_b']),
        'w_oa': p['w_oa'][l].astype(BF16), 'w_ob': p['w_ob'][l].astype(BF16), 'w_oc': p['w_oc'][l].astype(BF16),
        'w_out': p['w_out'][l].astype(BF16),
        'norm2_g': row(p['norm2_g']),
        'w_up_a': p['w_up_ffn'][l][:, :D_FF].astype(BF16), 'w_up_b': p['w_up_ffn'][l][:, D_FF:].astype(BF16),
        'ffn_conv_w': p['ffn_conv_w'][l], 'w_down': p['w_down'][l].astype(BF16),
    }


def _pad_rows_front(a, rows):
    return jnp.pad(a, ((0, 0), (rows - a.shape[1], 0), (0, 0)))


def _state_to_pairs(s):
    B = s.shape[0]
    a = jnp.swapaxes(s, -1, -2).reshape(B, N_PAIR, 2, D_HEAD, D_HEAD)
    z = jnp.zeros_like(a[:, :, 0])
    top = jnp.concatenate([a[:, :, 0], z], axis=-1)
    bot = jnp.concatenate([z, a[:, :, 1]], axis=-1)
    return jnp.concatenate([top, bot], axis=-2)


def _pairs_to_state(a):
    B = a.shape[0]
    even = a[:, :, :D_HEAD, :D_HEAD]
    odd = a[:, :, D_HEAD:, D_HEAD:]
    s = jnp.stack([even, odd], axis=2).reshape(B, N_HEAD, D_HEAD, D_HEAD)
    return jnp.swapaxes(s, -1, -2)


def _tile(t, pref):
    return pref if t % pref == 0 else t


def _layer(x, lw, attend, conv_hist, shift_hist, wkv0, ffn_hist, norm_f_g, final, prompt):
    B, T, _ = x.shape
    if prompt:
        qc, kc, k, v, vb, logf, u, p, gates = _in_proj(x, lw, True, _tile(T, 512))
        o_a = attend(qc, kc, vb)
    else:
        flat = x.reshape(1, B * T, D_MODEL)
        q, k, v, logf, u, p, gates = [a.reshape((B, T) + a.shape[2:]) for a in _in_proj(flat, lw, False, B * T)]
        o_a = attend(q, k, v, logf)
    if T >= CONV_W - 1:
        conv_new = u[:, T - (CONV_W - 1):]
    else:
        conv_new = jnp.concatenate([conv_hist[:, T:], u], axis=1)
    o_b = _conv(u, _pad_rows_front(conv_hist, HIST_ROWS), lw, _tile(T, 512))
    shift_new = p[:, -1:]
    r, lwd, k_eff, vc, kk, beta, g, bonus = _rwkv_prep(p, _pad_rows_front(shift_hist, 8), lw, _tile(T, 512))
    chunk = 64
    nb = 2 if B % 2 == 0 else 1
    if T % chunk:
        pad = lambda a: jnp.pad(a, ((0, 0), (0, chunk - T % chunk), (0, 0)))
        y, a_new = _rwkv_scan(pad(r), pad(lwd), pad(k_eff), pad(vc), pad(kk), pad(beta), _state_to_pairs(wkv0),
                              chunk, nb)
        y = y[:, :T]
    else:
        y, a_new = _rwkv_scan(r, lwd, k_eff, vc, kk, beta, _state_to_pairs(wkv0), chunk, nb)
    wkv_new = _pairs_to_state(a_new)
    if prompt:
        x = _merge(x, o_a, o_b, y, bonus, g, gates, lw, _tile(T, 512))
    else:
        fl = lambda a: a.reshape((1, B * T) + a.shape[2:])
        x = _merge(fl(x), fl(o_a), fl(o_b), fl(y), fl(bonus), fl(g), fl(gates), lw, B * T).reshape(B, T, D_MODEL)
    if prompt or T < FFN_CONV_W - 1:
        outs = _ffn(x, _pad_rows_front(ffn_hist, 8), lw, norm_f_g, final, _tile(T, 512))
        ffn_new = outs[1][:, 8 - (FFN_CONV_W - 1):]
    else:
        outs = _ffn_short(x, ffn_hist, lw, norm_f_g, final)
        ffn_new = outs[1]
    x = outs[0]
    xn = outs[2] if final else None
    state = (k.reshape(B, T, N_HEAD, D_HEAD), v.reshape(B, T, N_HEAD, D_HEAD), logf, conv_new, shift_new,
             wkv_new, ffn_new)
    return x, xn, state


def kernel(x_prompt, x_sample, cache_k, cache_v, cache_logf, state_conv, state_shift, state_wkv, state_ffn,
           page_table, norm1_g, w_in, b_f, b_gate, w_oa, conv_w, conv_b, conv_ln_g, conv_ln_b, w_ob,
           rwkv_mu, rwkv_w0, rwkv_w_lora_up, rwkv_a0, rwkv_a_lora_up, rwkv_g_lora_up, rwkv_k_k, rwkv_k_a,
           rwkv_r_k, rwkv_gn_g, rwkv_gn_b, w_oc, w_out, norm2_g, w_up_ffn, ffn_conv_w, w_down, norm_f_g):
    p = dict(norm1_g=norm1_g, w_in=w_in, b_f=b_f, b_gate=b_gate, w_oa=w_oa, conv_w=conv_w, conv_b=conv_b,
             conv_ln_g=conv_ln_g, conv_ln_b=conv_ln_b, w_ob=w_ob, rwkv_mu=rwkv_mu, rwkv_w0=rwkv_w0,
             rwkv_w_lora_up=rwkv_w_lora_up, rwkv_a0=rwkv_a0, rwkv_a_lora_up=rwkv_a_lora_up,
             rwkv_g_lora_up=rwkv_g_lora_up, rwkv_k_k=rwkv_k_k, rwkv_k_a=rwkv_k_a, rwkv_r_k=rwkv_r_k,
             rwkv_gn_g=rwkv_gn_g, rwkv_gn_b=rwkv_gn_b, w_oc=w_oc, w_out=w_out, norm2_g=norm2_g,
             w_up_ffn=w_up_ffn, ffn_conv_w=ffn_conv_w, w_down=w_down)
    depth = w_in.shape[0]
    nf = norm_f_g.reshape(1, D_MODEL)
    weights = [_layer_weights(l, p) for l in range(depth)]

    bp, tp, _ = x_prompt.shape
    bs, ts, _ = x_sample.shape
    page = cache_k.shape[2]

    yp, p_states = x_prompt, []
    for l in range(depth):
        attend = lambda qc, kc, vt: _fox_prompt(qc, kc, vt, _tile(tp, 512), _tile(tp, 1024))
        yp, ypn, st = _layer(yp, weights[l], attend,
                             jnp.zeros((bp, CONV_W - 1, W_HEADS), F32), jnp.zeros((bp, 1, C_SHIFT), F32),
                             jnp.zeros((bp, N_HEAD, D_HEAD, D_HEAD), F32), jnp.zeros((bp, FFN_CONV_W - 1, D_FF), F32),
                             nf, l == depth - 1, True)
        p_states.append(st)

    ys, s_states = x_sample, []
    n_pages = page_table.shape[1]
    n_group = 16 if n_pages % 16 == 0 else 1
    n_pool = cache_k.shape[1]
    kt_pool = jnp.transpose(cache_k, (0, 1, 3, 4, 2)).reshape(depth, n_pool, W_HEADS, page)
    vt_pool = jnp.transpose(cache_v, (0, 1, 3, 4, 2)).reshape(depth, n_pool, W_HEADS, page)
    lf_pool_t = jnp.swapaxes(cache_logf, 2, 3)
    for l in range(depth):

        def attend(q, k, v, logf, l=l):
            padk = lambda a: jnp.pad(a, ((0, 0), (0, page - ts), (0, 0)))
            lf_t = jnp.pad(jnp.swapaxes(logf, 1, 2), ((0, 0), (0, 0), (0, page - ts)))
            return _fox_sample(q, padk(k), padk(v), lf_t, l, kt_pool, vt_pool, lf_pool_t, page_table, n_group)

        ys, ysn, st = _layer(ys, weights[l], attend, state_conv[l], state_shift[l], state_wkv[l], state_ffn[l],
                             nf, l == depth - 1, False)
        s_states.append(st)

    stk = lambda states, i: jnp.stack([s[i] for s in states], axis=0)
    return (ypn, ysn,
            stk(p_states, 0), stk(p_states, 1), stk(p_states, 2), stk(p_states, 3),
            stk(p_states, 4), stk(p_states, 5), stk(p_states, 6),
            stk(s_states, 0), stk(s_states, 1), stk(s_states, 2), stk(s_states, 3),
            stk(s_states, 4), stk(s_states, 5), stk(s_states, 6))
```

```python
import functools

import numpy as np
import jax
import jax.numpy as jnp
from jax import lax
from jax.experimental import pallas as pl
from jax.experimental.pallas import tpu as pltpu

F32 = jnp.float32
BF16 = jnp.bfloat16

D_MODEL = 1024
N_HEAD = 8
D_HEAD = 64
W_HEADS = N_HEAD * D_HEAD
LANES = 128
N_PAIR = W_HEADS // LANES
C_SHIFT = 3 * W_HEADS + 64 + 64 + 128
D_FF = 2816
CONV_W = 31
FFN_CONV_W = 3
RMS_EPS = 1e-6
LN_EPS = 1e-5
GN_EPS = 64e-5
L2_EPS = 1e-12
NEG_BIG = -1e30
LOG2E = float(np.log2(np.e))
VMEM_LIMIT = 56 * 2**20


def _params(sem):
    return pltpu.CompilerParams(dimension_semantics=sem, vmem_limit_bytes=VMEM_LIMIT)


def _const_spec(shape):
    nd = len(shape)
    return pl.BlockSpec(shape, lambda *_: (0,) * nd, pipeline_mode=pl.Buffered(1))


def _dot(a, b):
    return jnp.dot(a.astype(BF16), b.astype(BF16), preferred_element_type=F32)


def _dot_nt(a, b):
    return lax.dot_general(a.astype(BF16), b.astype(BF16), (((1,), (1,)), ((), ())),
                           preferred_element_type=F32)


def _split3(x):
    hi = x.astype(BF16).astype(F32)
    r1 = x - hi
    mid = r1.astype(BF16).astype(F32)
    lo = (r1 - mid).astype(BF16).astype(F32)
    return hi, mid, lo


def _tri_dot(tri, x):
    pieces = jnp.concatenate(_split3(x), axis=1).astype(BF16)
    r = jnp.dot(tri, pieces, preferred_element_type=F32)
    return r[:, :LANES] + r[:, LANES:2 * LANES] + r[:, 2 * LANES:]


def _dot_tri(x, tri):
    pieces = jnp.concatenate(_split3(x), axis=0).astype(BF16)
    r = jnp.dot(pieces, tri, preferred_element_type=F32)
    n = x.shape[0]
    return r[:n] + r[n:2 * n] + r[2 * n:]


def _log_sigmoid(z):
    return jnp.minimum(z, 0.0) - jnp.log1p(jnp.exp(-jnp.abs(z)))


def _rmsnorm(x, g):
    return x * lax.rsqrt(jnp.mean(x * x, axis=-1, keepdims=True) + RMS_EPS) * g


def _inproj_body(prompt, x_ref, g_ref, wqkv_ref, wf_ref, bf_ref, wglu_ref, wr_ref, wg_ref, bg_ref, *rest):
    if prompt:
        (eplace_ref, qc_ref, kc_ref, k_ref, v_ref, vb_ref, lf_ref, u_ref, p_ref, gate_ref, fcarry) = rest
    else:
        (q_ref, k_ref, v_ref, lf_ref, u_ref, p_ref, gate_ref) = rest
    tm = x_ref.shape[1]
    hn = _rmsnorm(x_ref[0], g_ref[...]).astype(BF16)

    qkv = jnp.dot(hn, wqkv_ref[...], preferred_element_type=F32)
    q = qkv[:, :W_HEADS] * (D_HEAD ** -0.5 * (LOG2E if prompt else 1.0))
    k = qkv[:, W_HEADS:2 * W_HEADS]
    v = qkv[:, 2 * W_HEADS:]
    k_ref[0] = k
    v_ref[0] = v

    logf = _log_sigmoid(jnp.dot(hn, wf_ref[...], preferred_element_type=F32) + bf_ref[...])
    lf_ref[0] = logf[:, :N_HEAD]

    glu = jnp.dot(hn, wglu_ref[...], preferred_element_type=F32)
    u_ref[0] = glu[:, :W_HEADS] * jax.nn.sigmoid(glu[:, W_HEADS:])
    p_ref[0] = jnp.dot(hn, wr_ref[...], preferred_element_type=F32)
    gate_ref[0] = jax.nn.sigmoid(jnp.dot(hn, wg_ref[...], preferred_element_type=F32)
                                 + bg_ref[...]).astype(gate_ref.dtype)

    if not prompt:
        q_ref[0] = q.astype(q_ref.dtype)
        return

    v_t = v.T.astype(BF16)
    for pair in range(N_PAIR):
        vb_ref[0, pair] = v_t[pair * LANES:(pair + 1) * LANES, :]

    @pl.when(pl.program_id(1) == 0)
    def _():
        fcarry[...] = jnp.zeros_like(fcarry)

    lane = lax.broadcasted_iota(jnp.int32, (1, LANES), 1)
    lf = jnp.where(lane < N_HEAD, logf, 0.0)
    row = lax.broadcasted_iota(jnp.int32, (tm, tm), 0)
    col = lax.broadcasted_iota(jnp.int32, (tm, tm), 1)
    tri = jnp.where(col <= row, 1.0, 0.0).astype(BF16)
    fcum = _tri_dot(tri, lf) + fcarry[...]
    fcarry[...] = fcum[tm - 1:tm, :]
    aug = jnp.dot(jnp.concatenate(_split3(fcum * LOG2E), axis=1).astype(BF16), eplace_ref[...],
                  preferred_element_type=F32)
    slot = lane & 15
    augq = aug[:, :LANES] + jnp.where((slot >= 3) & (slot < 6), 1.0, 0.0)
    augk = aug[:, LANES:] + jnp.where(slot < 3, 1.0, 0.0)
    augk_b = augk.astype(BF16)
    for pair in range(N_PAIR):
        kc_ref[0, pair] = jnp.concatenate([k[:, pair * LANES:(pair + 1) * LANES].astype(BF16), augk_b], axis=1)
        q_pair = q[:, pair * LANES:(pair + 1) * LANES]
        for par in range(2):
            h = 2 * pair + par
            q_h = jnp.where((lane >= D_HEAD) if par else (lane < D_HEAD), q_pair, 0.0)
            a_h = jnp.where((lane >> 4) == h, augq, 0.0)
            qc_ref[0, h] = jnp.concatenate([q_h, a_h], axis=1).astype(BF16)


def _eplace():
    e = np.zeros((3 * LANES, 2 * LANES), np.float32)
    for piece in range(3):
        for h in range(N_HEAD):
            e[piece * LANES + h, 16 * h + piece] = 1.0
            e[piece * LANES + h, LANES + 16 * h + 3 + piece] = -1.0
    return jnp.asarray(e, BF16)


def _in_proj(x, lw, prompt, tm):
    B, T, _ = x.shape
    nt = T // tm
    row = lambda width: pl.BlockSpec((1, tm, width), lambda b, i: (b, i, 0))
    weights = [lw['norm1_g'], lw['w_qkv'], lw['w_f'], lw['b_f'], lw['w_glu'], lw['w_r'], lw['w_g'], lw['b_gate']]
    in_specs = [row(D_MODEL)] + [_const_spec(w.shape) for w in weights]
    common_shapes = [
        jax.ShapeDtypeStruct((B, T, W_HEADS), F32),
        jax.ShapeDtypeStruct((B, T, W_HEADS), F32),
    ]
    tail_shapes = [
        jax.ShapeDtypeStruct((B, T, N_HEAD), F32),
        jax.ShapeDtypeStruct((B, T, W_HEADS), F32),
        jax.ShapeDtypeStruct((B, T, C_SHIFT), F32),
        jax.ShapeDtypeStruct((B, T, 3 * D_MODEL), BF16),
    ]
    tail_specs = [row(N_HEAD), row(W_HEADS), row(C_SHIFT), row(3 * D_MODEL)]
    if prompt:
        args = [x] + weights + [_eplace()]
        in_specs = in_specs + [_const_spec((3 * LANES, 2 * LANES))]
        out_shape = [jax.ShapeDtypeStruct((B, N_HEAD, T, 2 * LANES), BF16),
                     jax.ShapeDtypeStruct((B, N_PAIR, T, 2 * LANES), BF16)] + common_shapes + \
                    [jax.ShapeDtypeStruct((B, N_PAIR, LANES, T), BF16)] + tail_shapes
        out_specs = [pl.BlockSpec((1, N_HEAD, tm, 2 * LANES), lambda b, i: (b, 0, i, 0)),
                     pl.BlockSpec((1, N_PAIR, tm, 2 * LANES), lambda b, i: (b, 0, i, 0)),
                     row(W_HEADS), row(W_HEADS),
                     pl.BlockSpec((1, N_PAIR, LANES, tm), lambda b, i: (b, 0, 0, i))] + tail_specs
        scratch = [pltpu.VMEM((1, LANES), F32)]
    else:
        args = [x] + weights
        out_shape = [jax.ShapeDtypeStruct((B, T, W_HEADS), BF16)] + common_shapes + tail_shapes
        out_specs = [row(W_HEADS), row(W_HEADS), row(W_HEADS)] + tail_specs
        scratch = []
    return pl.pallas_call(
        functools.partial(_inproj_body, prompt),
        grid=(B, nt), in_specs=in_specs, out_specs=out_specs, out_shape=out_shape,
        scratch_shapes=scratch, compiler_params=_params(("arbitrary", "arbitrary")),
        name="in_proj_prompt" if prompt else "in_proj_sample",
    )(*args)


ONES_ROWS = 16


def _fox_prompt_body(qc_ref, kc_ref, vt_ref, o_ref, v1t, *, tq, tk):
    i = pl.program_id(2)
    n_pp = kc_ref.shape[1]
    heads = range(2 * n_pp)
    n_kt = v1t.shape[1]

    @pl.when(i == 0)
    def _():
        for pp in range(n_pp):
            for j in range(n_kt):
                v1t[pp, j, :LANES, :] = vt_ref[0, pp, :, j * tk:(j + 1) * tk]
                v1t[pp, j, LANES:, :] = jnp.ones((ONES_ROWS, tk), BF16)

    qs = [qc_ref[0, h] for h in heads]

    def tile(j, carry, masked):
        start = pl.multiple_of(j * tk, tk)
        kcs = [kc_ref[0, pp, pl.ds(start, tk), :] for pp in range(n_pp)]
        vts = [v1t[pp, j] for pp in range(n_pp)]
        ss = [lax.dot_general(kcs[h // 2], qs[h], (((1,), (1,)), ((), ())), preferred_element_type=F32)
              for h in heads]
        if masked:
            key = start + lax.broadcasted_iota(jnp.int32, (tk, tq), 0)
            qpos = i * tq + lax.broadcasted_iota(jnp.int32, (tk, tq), 1)
            ss = [jnp.where(key <= qpos, s, NEG_BIG) for s in ss]
        out = []
        for pp in range(n_pp):
            pair = (2 * pp, 2 * pp + 1)
            m_new = [jnp.maximum(carry[h][0], jnp.max(ss[h], axis=0, keepdims=True)) for h in pair]
            ps = [jnp.exp2(ss[h] - m).astype(BF16) for h, m in zip(pair, m_new)]
            accs = [jnp.exp2(carry[h][0] - m) * carry[h][1] + jnp.dot(vts[pp], p, preferred_element_type=F32)
                    for h, m, p in zip(pair, m_new, ps)]
            out.extend(zip(m_new, accs))
        return tuple(out)

    init = tuple((jnp.full((1, tq), NEG_BIG, F32), jnp.zeros((LANES + ONES_ROWS, tq), F32)) for _ in heads)
    n_full = (i * tq) // tk
    carry = lax.fori_loop(0, n_full, lambda j, c: tile(j, c, False), init)
    final = tile(n_full, carry, True)
    chan = lax.broadcasted_iota(jnp.int32, (LANES, 1), 0)
    for pp in range(n_pp):
        a0, a1 = final[2 * pp][1], final[2 * pp + 1][1]
        o_t = jnp.where(chan < D_HEAD, a0[:LANES] / a0[LANES:LANES + 1], a1[:LANES] / a1[LANES:LANES + 1])
        o_ref[0, :, pp * LANES:(pp + 1) * LANES] = o_t.T.astype(o_ref.dtype)


def _fox_prompt(qc, kc, vt, tq, tk, n_pp=2):
    B, _, T, _ = qc.shape
    return pl.pallas_call(
        functools.partial(_fox_prompt_body, tq=tq, tk=tk),
        grid=(B, N_PAIR // n_pp, T // tq),
        in_specs=[pl.BlockSpec((1, 2 * n_pp, tq, 2 * LANES), lambda b, p, i: (b, p, i, 0)),
                  pl.BlockSpec((1, n_pp, T, 2 * LANES), lambda b, p, i: (b, p, 0, 0)),
                  pl.BlockSpec((1, n_pp, LANES, T), lambda b, p, i: (b, p, 0, 0))],
        out_specs=pl.BlockSpec((1, tq, n_pp * LANES), lambda b, p, i: (b, i, p)),
        out_shape=jax.ShapeDtypeStruct((B, T, W_HEADS), BF16),
        scratch_shapes=[pltpu.VMEM((n_pp, T // tk, LANES + ONES_ROWS, tk), BF16)],
        compiler_params=_params(("arbitrary", "arbitrary", "arbitrary")),
        name="fox_prompt",
    )(qc, kc, vt)


def _fox_sample_body(n_group, t_new, pt_ref, q_ref, kn_ref, vn_ref, lfn_ref, *rest):
    k_refs = rest[:n_group]
    v_refs = rest[n_group:2 * n_group]
    lf_refs = rest[2 * n_group:3 * n_group]
    o_ref, qbd, m_s, l_s, acc_s, tail_s, newsum_s = rest[3 * n_group:]
    j = pl.program_id(1)
    nrow = t_new * N_HEAD
    page = lf_refs[0].shape[1]
    lane_w = lax.broadcasted_iota(jnp.int32, (N_HEAD, W_HEADS), 1)
    head_w = lax.broadcasted_iota(jnp.int32, (N_HEAD, W_HEADS), 0)
    bd_mask = (lane_w >> 6) == head_w
    ri = lax.broadcasted_iota(jnp.int32, (page, page), 0)
    ci = lax.broadcasted_iota(jnp.int32, (page, page), 1)

    def tile4(x):
        return jnp.concatenate([x] * t_new, axis=0)

    def online(s, pv_fn):
        m_new = jnp.maximum(m_s[...], jnp.max(s, axis=-1, keepdims=True))
        alpha = jnp.exp(m_s[...] - m_new)
        p = jnp.exp(s - m_new)
        l_s[...] = alpha * l_s[...] + jnp.sum(p, axis=-1, keepdims=True)
        acc_s[...] = alpha * acc_s[...] + pv_fn(p.astype(BF16))
        m_s[...] = m_new

    @pl.when(j == 0)
    def _():
        q = q_ref[0].astype(F32)
        qbd[...] = jnp.concatenate(
            [jnp.where(bd_mask, jnp.broadcast_to(q[t:t + 1, :], (N_HEAD, W_HEADS)), 0.0)
             for t in range(t_new)], axis=0).astype(BF16)
        m_s[...] = jnp.full_like(m_s, NEG_BIG)
        l_s[...] = jnp.zeros_like(l_s)
        acc_s[...] = jnp.zeros_like(acc_s)
        tail_s[...] = jnp.zeros_like(tail_s)
        tri_incl = jnp.where(ri <= ci, 1.0, 0.0).astype(BF16)
        cum = _dot_tri(lfn_ref[0], tri_incl)
        newsum = jnp.concatenate([cum[:, t:t + 1] for t in range(t_new)], axis=0)
        newsum_s[...] = newsum
        s = _dot_nt(qbd[...], kn_ref[0])
        rr = lax.broadcasted_iota(jnp.int32, (nrow, page), 0) >> 3
        cc = lax.broadcasted_iota(jnp.int32, (nrow, page), 1)
        s = jnp.where(cc <= rr, s + newsum - tile4(cum), NEG_BIG)
        online(s, lambda p: _dot(p, vn_ref[0]))

    tri_after = jnp.where(ri > ci, 1.0, 0.0).astype(BF16)
    lfs = [lf_refs[g][...] for g in range(n_group)]
    pieces = jnp.concatenate([jnp.concatenate(_split3(lf), axis=0) for lf in lfs], axis=0).astype(BF16)
    suffix = jnp.dot(pieces, tri_after, preferred_element_type=F32)
    qb = qbd[...]
    tail = tail_s[...]
    parts = []
    for g in range(n_group):
        r0 = 3 * N_HEAD * g
        bias8 = (suffix[r0:r0 + N_HEAD] + suffix[r0 + N_HEAD:r0 + 2 * N_HEAD]
                 + suffix[r0 + 2 * N_HEAD:r0 + 3 * N_HEAD] + tail)
        parts.append(_dot(qb, k_refs[g][...]) + tile4(bias8))
        tail = tail + jnp.sum(lfs[g], axis=-1, keepdims=True)
    tail_s[...] = tail
    s_all = jnp.concatenate(parts, axis=1) + newsum_s[...]

    def pv_pages(p):
        acc = _dot_nt(p[:, :page], v_refs[0][...])
        for g in range(1, n_group):
            acc = acc + _dot_nt(p[:, g * page:(g + 1) * page], v_refs[g][...])
        return acc

    online(s_all, pv_pages)

    @pl.when(j == pl.num_programs(1) - 1)
    def _():
        o = acc_s[...] / l_s[...]
        for t in range(t_new):
            o_ref[0, t:t + 1, :] = jnp.sum(jnp.where(bd_mask, o[t * N_HEAD:(t + 1) * N_HEAD, :], 0.0), axis=0,
                                           keepdims=True).astype(o_ref.dtype)


def _fox_sample(q, k_new, v_new, lf_new_t, layer, kt_pool, vt_pool, lf_pool_t, page_table, n_group):
    B, t_new, _ = q.shape
    page = kt_pool.shape[3]
    n_pages = page_table.shape[1]
    steps = n_pages // n_group

    def paged(width_shape):
        specs = []
        for g in range(n_group):
            specs.append(pl.BlockSpec((None, None) + width_shape,
                                      lambda b, j, pt, g=g: (layer, pt[b, n_pages - 1 - (j * n_group + g)], 0, 0)))
        return specs

    per_b = lambda shape: pl.BlockSpec((1,) + shape, lambda b, j, pt: (b, 0, 0))
    nrow = t_new * N_HEAD
    grid_spec = pltpu.PrefetchScalarGridSpec(
        num_scalar_prefetch=1, grid=(B, steps),
        in_specs=[per_b((t_new, W_HEADS)), per_b((page, W_HEADS)), per_b((page, W_HEADS)), per_b((N_HEAD, page))]
        + paged((W_HEADS, page)) + paged((W_HEADS, page)) + paged((N_HEAD, page)),
        out_specs=per_b((t_new, W_HEADS)),
        scratch_shapes=[pltpu.VMEM((nrow, W_HEADS), BF16), pltpu.VMEM((nrow, 1), F32), pltpu.VMEM((nrow, 1), F32),
                        pltpu.VMEM((nrow, W_HEADS), F32), pltpu.VMEM((N_HEAD, 1), F32), pltpu.VMEM((nrow, 1), F32)])
    return pl.pallas_call(
        functools.partial(_fox_sample_body, n_group, t_new),
        grid_spec=grid_spec,
        out_shape=jax.ShapeDtypeStruct((B, t_new, W_HEADS), F32),
        compiler_params=_params(("arbitrary", "arbitrary")),
        name="fox_sample",
    )(page_table, q, k_new, v_new, lf_new_t, *([kt_pool] * n_group), *([vt_pool] * n_group),
      *([lf_pool_t] * n_group))


HIST_ROWS = 32


SUBLANES = 8
CONV_ROWS = 32


def _conv_body(u_ref, hist_ref, w_ref, b_ref, g_ref, beta_ref, o_ref, ext, shifted):
    tt = u_ref.shape[1]

    @pl.when(pl.program_id(1) == 0)
    def _():
        ext[0:HIST_ROWS, :] = hist_ref[0]

    ext[HIST_ROWS:HIST_ROWS + tt, :] = u_ref[0]
    span = tt + HIST_ROWS - SUBLANES
    for s in range(1, SUBLANES):
        shifted[s - 1, 0:span, :] = ext[s:s + span, :]
    first = HIST_ROWS - (CONV_W - 1)
    rows = min(tt, CONV_ROWS)
    for c0 in range(0, tt, rows):
        acc = jnp.zeros((rows, W_HEADS), F32)
        for tap in range(CONV_W):
            s = (first + tap) % SUBLANES
            base = first + tap - s + c0
            win = ext[base:base + rows, :] if s == 0 else shifted[s - 1, base:base + rows, :]
            acc = acc + w_ref[tap:tap + 1, :] * win
        c = acc + b_ref[...]
        mean = jnp.mean(c, axis=-1, keepdims=True)
        xc = c - mean
        var = jnp.mean(xc * xc, axis=-1, keepdims=True)
        y = xc * lax.rsqrt(var + LN_EPS) * g_ref[...] + beta_ref[...]
        o_ref[0, c0:c0 + rows, :] = (y * jax.nn.sigmoid(y)).astype(o_ref.dtype)
    ext[0:HIST_ROWS, :] = ext[tt:tt + HIST_ROWS, :]


def _conv(u, hist, lw, tt):
    B, T, _ = u.shape
    return pl.pallas_call(
        _conv_body, grid=(B, T // tt),
        in_specs=[pl.BlockSpec((1, tt, W_HEADS), lambda b, i: (b, i, 0)),
                  pl.BlockSpec((1, HIST_ROWS, W_HEADS), lambda b, i: (b, 0, 0)),
                  _const_spec((CONV_W, W_HEADS)), _const_spec((1, W_HEADS)), _const_spec((1, W_HEADS)),
                  _const_spec((1, W_HEADS))],
        out_specs=pl.BlockSpec((1, tt, W_HEADS), lambda b, i: (b, i, 0)),
        out_shape=jax.ShapeDtypeStruct((B, T, W_HEADS), BF16),
        scratch_shapes=[pltpu.VMEM((HIST_ROWS + tt, W_HEADS), F32),
                        pltpu.VMEM((SUBLANES - 1, HIST_ROWS + tt - SUBLANES, W_HEADS), F32)],
        compiler_params=_params(("arbitrary", "arbitrary")),
        name="conv",
    )(u, hist, lw['conv_w'], lw['conv_b'], lw['conv_ln_g'], lw['conv_ln_b'])


def _rwkv_prep_body(p_ref, hist_ref, mu_ref, w0_ref, a0_ref, kk_ref, ka_ref, rk_ref, wwa_ref, wg_ref, bd_ref,
                    r_o, lw_o, k_o, v_o, kk_o, b_o, g_o, bonus_o, ext):
    tt = p_ref.shape[1]

    @pl.when(pl.program_id(1) == 0)
    def _():
        ext[0:8, :] = hist_ref[0]

    p = p_ref[0]
    ext[8:8 + tt, :] = p
    prev = ext[7:7 + tt, :]
    ps = p + (prev - p) * mu_ref[...]
    ext[0:8, :] = ext[tt:tt + 8, :]

    r = ps[:, :W_HEADS]
    kc = ps[:, W_HEADS:2 * W_HEADS]
    vc = ps[:, 2 * W_HEADS:3 * W_HEADS]
    wa = ps[:, 3 * W_HEADS:3 * W_HEADS + LANES]
    gd = ps[:, 3 * W_HEADS + LANES:]
    lane = lax.broadcasted_iota(jnp.int32, (1, LANES), 1)
    wa = jnp.where(lane < 64, jnp.tanh(wa), wa)
    lora = _dot(wa, wwa_ref[...])
    lw_o[0] = -float(np.exp(-0.5)) * jax.nn.sigmoid(w0_ref[...] + lora[:, :W_HEADS])
    a = jax.nn.sigmoid(a0_ref[...] + lora[:, W_HEADS:])
    g_o[0] = _dot(jax.nn.sigmoid(gd), wg_ref[...])
    bd = bd_ref[...]
    kk = kc * kk_ref[...]
    sq = kk * kk
    sq_hi = sq.astype(BF16)
    sq_lo = (sq - sq_hi.astype(F32)).astype(BF16)
    ss = jnp.dot(sq_hi, bd, preferred_element_type=F32) + jnp.dot(sq_lo, bd, preferred_element_type=F32)
    kk = kk / jnp.maximum(jnp.sqrt(ss), L2_EPS)
    k_eff = kc * (1.0 + (a - 1.0) * ka_ref[...])
    r_o[0] = r
    k_o[0] = k_eff
    v_o[0] = vc
    kk_o[0] = kk
    b_o[0] = kk * a
    bonus_o[0] = _dot(r * k_eff * rk_ref[...], bd) * vc


def _rwkv_prep(p, hist, lw, tt):
    B, T, _ = p.shape
    row = lambda width: pl.BlockSpec((1, tt, width), lambda b, i: (b, i, 0))
    consts = [lw['rwkv_mu'], lw['rwkv_w0'], lw['rwkv_a0'], lw['rwkv_k_k'], lw['rwkv_k_a'], lw['rwkv_r_k'],
              lw['w_wa'], lw['rwkv_g_lora_up'], lw['bd_ones']]
    return pl.pallas_call(
        _rwkv_prep_body, grid=(B, T // tt),
        in_specs=[row(C_SHIFT), pl.BlockSpec((1, 8, C_SHIFT), lambda b, i: (b, 0, 0))]
        + [_const_spec(c.shape) for c in consts],
        out_specs=[row(W_HEADS)] * 8,
        out_shape=[jax.ShapeDtypeStruct((B, T, W_HEADS), F32)] * 8,
        scratch_shapes=[pltpu.VMEM((8 + tt, C_SHIFT), F32)],
        compiler_params=_params(("arbitrary", "arbitrary")),
        name="rwkv_prep",
    )(p, hist, *consts)


def _rwkv_scan_body(r_ref, lw_ref, k_ref, v_ref, kk_ref, b_ref, a0_ref, y_ref, a_out_ref, a_s, *, chunk, nb):
    c = chunk
    n = 2 * c

    @pl.when(pl.program_id(1) == 0)
    def _():
        a_s[...] = a0_ref[...]

    ri = lax.broadcasted_iota(jnp.int32, (n, n), 0)
    ci = lax.broadcasted_iota(jnp.int32, (n, n), 1)
    strict = ri > ci
    incl = ri >= ci
    eye_f = jnp.where(ri == ci, 1.0, 0.0)
    tr = lax.broadcasted_iota(jnp.int32, (c, c), 0)
    tc = lax.broadcasted_iota(jnp.int32, (c, c), 1)
    tri_incl = jnp.where(tc <= tr, 1.0, 0.0).astype(BF16)
    lane = lax.broadcasted_iota(jnp.int32, (1, LANES), 1)
    m_even = lane < D_HEAD

    def stack(x):
        return jnp.concatenate([jnp.where(m_even, x, 0.0), jnp.where(m_even, 0.0, x)], axis=0)

    units = [(bi, pair) for bi in range(nb) for pair in range(N_PAIR)]

    def load(ref):
        return [ref[bi, :, pair * LANES:(pair + 1) * LANES] for bi, pair in units]

    lw = load(lw_ref)
    cum = [_tri_dot(tri_incl, x) for x in lw]
    cum_end = [x[c - 1:c, :] for x in cum]
    e_neg = [jnp.exp(-x) for x in cum]
    e_end = [jnp.exp(ce - x) for ce, x in zip(cum_end, cum)]
    r, k, v, kk, b = load(r_ref), load(k_ref), load(v_ref), load(kk_ref), load(b_ref)
    rt = [stack(x * jnp.exp(cu)).astype(BF16) for x, cu in zip(r, cum)]
    ka = [stack(x * jnp.exp(cu - l)).astype(BF16) for x, cu, l in zip(kk, cum, lw)]
    kt = [stack(x * e).astype(BF16) for x, e in zip(k, e_neg)]
    bt = [stack(x * e).astype(BF16) for x, e in zip(b, e_neg)]
    kh_t = [stack(x * e).T.astype(BF16) for x, e in zip(k, e_end)]
    bh_t = [stack(x * e).T.astype(BF16) for x, e in zip(b, e_end)]
    vs = [stack(x).astype(BF16) for x in v]

    big = [_dot_nt(jnp.concatenate([a, q], axis=0), jnp.concatenate([x, y], axis=0))
           for a, q, x, y in zip(ka, rt, kt, bt)]
    m_mat = [jnp.where(strict, x[:n, :n], 0.0).astype(BF16) for x in big]
    l_mat = [jnp.where(strict, x[:n, n:], 0.0) for x in big]
    n1 = [jnp.where(incl, x[n:, :n], 0.0).astype(BF16) for x in big]
    n2 = [jnp.where(incl, x[n:, n:], 0.0).astype(BF16) for x in big]

    inv = [eye_f - jnp.where((ri >> 1) == (ci >> 1), x, 0.0) for x in l_mat]
    size = 2
    shift = 1
    while size < c:
        blk = ((ri >> (shift + 1)) == (ci >> (shift + 1))) & (((ri >> shift) & 1) == 1) & (((ci >> shift) & 1) == 0)
        inv_b = [x.astype(BF16) for x in inv]
        half = [_dot(x, jnp.where(blk, l, 0.0)) for x, l in zip(inv_b, l_mat)]
        inv = [x - _dot(h, xb) for x, h, xb in zip(inv, half, inv_b)]
        size *= 2
        shift += 1
    inv_b = [x.astype(BF16) for x in inv]

    mv = [_dot(m, x).astype(BF16) for m, x in zip(m_mat, vs)]
    wu = [_dot(i, jnp.concatenate([a, x], axis=1)).astype(BF16) for i, a, x in zip(inv_b, ka, mv)]
    nw = [_dot(x, w) for x, w in zip(n2, wu)]
    rq = [q.astype(F32) - x[:, :LANES] for q, x in zip(rt, nw)]
    y0 = [_dot(m, x) - w[:, LANES:] for m, x, w in zip(n1, vs, nw)]
    bw = [_dot(x, w) for x, w in zip(bh_t, wu)]
    eye_l = lax.broadcasted_iota(jnp.int32, (LANES, LANES), 0) == lax.broadcasted_iota(jnp.int32, (LANES, LANES), 1)
    g_mat = [jnp.where(eye_l, jnp.broadcast_to(jnp.exp(ce), (LANES, LANES)), 0.0) - x[:, :LANES]
             for ce, x in zip(cum_end, bw)]
    h_mat = [_dot(x, y) - w[:, LANES:] for x, y, w in zip(kh_t, vs, bw)]
    a_prev = [a_s[bi, pair].astype(BF16) for bi, pair in units]
    ys = [_dot(q, a) + y for q, a, y in zip(rq, a_prev, y0)]
    a_new = [_dot(g, a) + h for g, a, h in zip(g_mat, a_prev, h_mat)]
    for (bi, pair), y, a in zip(units, ys, a_new):
        y_ref[bi, :, pair * LANES:(pair + 1) * LANES] = y[:c] + y[c:]
        a_s[bi, pair] = a

    @pl.when(pl.program_id(1) == pl.num_programs(1) - 1)
    def _():
        a_out_ref[...] = a_s[...]


def _rwkv_scan(r, lwd, k, v, kk, b, a0, chunk, nb):
    B, T, _ = r.shape
    row = pl.BlockSpec((nb, chunk, W_HEADS), lambda bb, i: (bb, i, 0))
    st = pl.BlockSpec((nb, N_PAIR, LANES, LANES), lambda bb, i: (bb, 0, 0, 0))
    return pl.pallas_call(
        functools.partial(_rwkv_scan_body, chunk=chunk, nb=nb), grid=(B // nb, T // chunk),
        in_specs=[row] * 6 + [st], out_specs=[row, st],
        out_shape=[jax.ShapeDtypeStruct((B, T, W_HEADS), F32),
                   jax.ShapeDtypeStruct((B, N_PAIR, LANES, LANES), F32)],
        scratch_shapes=[pltpu.VMEM((nb, N_PAIR, LANES, LANES), F32)],
        compiler_params=_params(("arbitrary", "arbitrary")),
        name="rwkv_scan",
    )(r, lwd, k, v, kk, b, a0)


def _merge_body(x_ref, oa_ref, ob_ref, y_ref, bonus_ref, g_ref, gate_ref, woa_ref, wob_ref, woc_ref, wout_ref,
                gng_ref, gnb_ref, bd_ref, o_ref):
    bd = bd_ref[...]
    y = y_ref[0]
    mean = _dot(y, bd) * (1.0 / D_HEAD)
    xc = y - mean
    var = _dot(xc * xc, bd) * (1.0 / D_HEAD)
    yn = xc * lax.rsqrt(var + GN_EPS) * gng_ref[...] + gnb_ref[...]
    oc = (yn + bonus_ref[0]) * g_ref[0]
    gates = gate_ref[0].astype(F32)
    merged = (gates[:, :D_MODEL] * _dot(oa_ref[0], woa_ref[...])
              + gates[:, D_MODEL:2 * D_MODEL] * _dot(ob_ref[0], wob_ref[...])
              + gates[:, 2 * D_MODEL:] * _dot(oc, woc_ref[...]))
    o_ref[0] = x_ref[0] + _dot(merged, wout_ref[...])


def _merge(x, oa, ob, y, bonus, g, gates, lw, tm):
    B, T, _ = x.shape
    row = lambda width: pl.BlockSpec((1, tm, width), lambda b, i: (b, i, 0))
    consts = [lw['w_oa'], lw['w_ob'], lw['w_oc'], lw['w_out'], lw['rwkv_gn_g'], lw['rwkv_gn_b'], lw['bd_ones']]
    return pl.pallas_call(
        _merge_body, grid=(B, T // tm),
        in_specs=[row(D_MODEL), row(W_HEADS), row(W_HEADS), row(W_HEADS), row(W_HEADS), row(W_HEADS),
                  row(3 * D_MODEL)] + [_const_spec(c.shape) for c in consts],
        out_specs=row(D_MODEL), out_shape=jax.ShapeDtypeStruct((B, T, D_MODEL), F32),
        compiler_params=_params(("arbitrary", "arbitrary")),
        name="merge",
    )(x, oa, ob, y, bonus, g, gates, *consts)


def _gelu_tanh(x):
    return 0.5 * x * (1.0 + jnp.tanh(float(np.sqrt(2.0 / np.pi)) * (x + 0.044715 * (x * x * x))))


def _ffn_body(final, x_ref, hist_ref, g2_ref, wa_ref, wb_ref, cw_ref, wd_ref, gf_ref, *rest):
    if final:
        o_ref, new_ref, yn_ref, ext = rest
    else:
        o_ref, new_ref, ext = rest
    tm = x_ref.shape[1]

    @pl.when(pl.program_id(1) == 0)
    def _():
        ext[0:8, :] = hist_ref[0]

    x = x_ref[0]
    hn = _rmsnorm(x, g2_ref[...]).astype(BF16)
    ext[8:8 + tm, :] = jnp.dot(hn, wa_ref[...], preferred_element_type=F32)
    ub = jnp.dot(hn, wb_ref[...], preferred_element_type=F32)
    c2 = (cw_ref[0:1, :] * ext[6:6 + tm, :] + cw_ref[1:2, :] * ext[7:7 + tm, :]
          + cw_ref[2:3, :] * ext[8:8 + tm, :])
    out = x + _dot(_gelu_tanh(c2) * ub, wd_ref[...])
    o_ref[0] = out
    last = ext[tm:tm + 8, :]
    new_ref[0] = last
    ext[0:8, :] = last
    if final:
        yn_ref[0] = _rmsnorm(out, gf_ref[...])


def _ffn_short_body(final, t_seq, x_ref, prev_ref, g2_ref, wa_ref, wb_ref, cw_ref, wd_ref, gf_ref, *rest):
    if final:
        o_ref, ua_ref, yn_ref, ext = rest
    else:
        o_ref, ua_ref, ext = rest
    tm = x_ref.shape[1]
    x = x_ref[0]
    hn = _rmsnorm(x, g2_ref[...]).astype(BF16)
    ua = jnp.dot(hn, wa_ref[...], preferred_element_type=F32)
    ub = jnp.dot(hn, wb_ref[...], preferred_element_type=F32)
    ua_ref[0] = ua
    ext[0:8, :] = jnp.zeros((8, D_FF), F32)
    ext[8:8 + tm, :] = ua
    t = lax.rem(lax.broadcasted_iota(jnp.int32, (tm, 1), 0), t_seq)
    back1 = jnp.where(t >= 1, ext[7:7 + tm, :], prev_ref[0])
    back2 = jnp.where(t >= 2, ext[6:6 + tm, :], prev_ref[1])
    c2 = cw_ref[0:1, :] * back2 + cw_ref[1:2, :] * back1 + cw_ref[2:3, :] * ua
    out = x + _dot(_gelu_tanh(c2) * ub, wd_ref[...])
    o_ref[0] = out
    if final:
        yn_ref[0] = _rmsnorm(out, gf_ref[...])


def _ffn_short(x, hist, lw, norm_f_g, final):
    B, T, _ = x.shape
    M = B * T
    zeros = jnp.zeros((B, T, D_FF), F32)
    prev1 = zeros.at[:, 0].set(hist[:, 1])
    prev2 = zeros.at[:, 0].set(hist[:, 0]).at[:, 1].set(hist[:, 1])
    prev = jnp.stack([prev1.reshape(M, D_FF), prev2.reshape(M, D_FF)])
    row = _const_spec((1, M, D_MODEL))
    wide = _const_spec((1, M, D_FF))
    consts = [lw['norm2_g'], lw['w_up_a'], lw['w_up_b'], lw['ffn_conv_w'], lw['w_down'], norm_f_g]
    out_specs = [row, wide]
    out_shape = [jax.ShapeDtypeStruct((1, M, D_MODEL), F32), jax.ShapeDtypeStruct((1, M, D_FF), F32)]
    if final:
        out_specs.append(row)
        out_shape.append(jax.ShapeDtypeStruct((1, M, D_MODEL), F32))
    outs = pl.pallas_call(
        functools.partial(_ffn_short_body, final, T), grid=(1,),
        in_specs=[row, _const_spec((2, M, D_FF))] + [_const_spec(c.shape) for c in consts],
        out_specs=out_specs, out_shape=out_shape,
        scratch_shapes=[pltpu.VMEM((8 + M, D_FF), F32)],
        compiler_params=_params(("arbitrary",)),
        name="ffn_short",
    )(x.reshape(1, M, D_MODEL), prev, *consts)
    res = [outs[0].reshape(B, T, D_MODEL), outs[1].reshape(B, T, D_FF)[:, T - (FFN_CONV_W - 1):]]
    if final:
        res.append(outs[2].reshape(B, T, D_MODEL))
    return res


def _ffn(x, hist, lw, norm_f_g, final, tm):
    B, T, _ = x.shape
    row = pl.BlockSpec((1, tm, D_MODEL), lambda b, i: (b, i, 0))
    hist_spec = pl.BlockSpec((1, 8, D_FF), lambda b, i: (b, 0, 0))
    consts = [lw['norm2_g'], lw['w_up_a'], lw['w_up_b'], lw['ffn_conv_w'], lw['w_down'], norm_f_g]
    out_specs = [row, hist_spec]
    out_shape = [jax.ShapeDtypeStruct((B, T, D_MODEL), F32), jax.ShapeDtypeStruct((B, 8, D_FF), F32)]
    if final:
        out_specs.append(row)
        out_shape.append(jax.ShapeDtypeStruct((B, T, D_MODEL), F32))
    return pl.pallas_call(
        functools.partial(_ffn_body, final), grid=(B, T // tm),
        in_specs=[row, hist_spec] + [_const_spec(c.shape) for c in consts],
        out_specs=out_specs, out_shape=out_shape,
        scratch_shapes=[pltpu.VMEM((8 + tm, D_FF), F32)],
        compiler_params=_params(("arbitrary", "arbitrary")),
        name="ffn",
    )(x, hist, *consts)


def _layer_weights(l, p):
    w_in = p['w_in'][l]
    o_f = 3 * W_HEADS
    o_glu = o_f + N_HEAD
    o_r = o_glu + 2 * W_HEADS
    o_g = o_r + C_SHIFT
    row = lambda a: a[l].reshape(1, -1)
    wwa = jnp.zeros((LANES, 2 * W_HEADS), F32)
    wwa = wwa.at[:64, :W_HEADS].set(p['rwkv_w_lora_up'][l]).at[64:, W_HEADS:].set(p['rwkv_a_lora_up'][l])
    head = np.arange(W_HEADS) // D_HEAD
    return {
        'norm1_g': row(p['norm1_g']),
        'w_qkv': w_in[:, :o_f].astype(BF16),
        'w_f': jnp.pad(w_in[:, o_f:o_glu], ((0, 0), (0, LANES - N_HEAD))).astype(BF16),
        'b_f': jnp.pad(p['b_f'][l], (0, LANES - N_HEAD)).reshape(1, LANES),
        'w_glu': w_in[:, o_glu:o_r].astype(BF16),
        'w_r': w_in[:, o_r:o_g].astype(BF16),
        'w_g': w_in[:, o_g:].astype(BF16),
        'b_gate': row(p['b_gate']),
        'conv_w': p['conv_w'][l], 'conv_b': row(p['conv_b']), 'conv_ln_g': row(p['conv_ln_g']),
        'conv_ln_b': row(p['conv_ln_b']),
        'rwkv_mu': row(p['rwkv_mu']), 'rwkv_w0': row(p['rwkv_w0']), 'rwkv_a0': row(p['rwkv_a0']),
        'rwkv_k_k': row(p['rwkv_k_k']), 'rwkv_k_a': row(p['rwkv_k_a']), 'rwkv_r_k': row(p['rwkv_r_k']),
        'w_wa': wwa.astype(BF16), 'rwkv_g_lora_up': p['rwkv_g_lora_up'][l].astype(BF16),
        'bd_ones': jnp.asarray(head[:, None] == head[None, :], BF16),
        'rwkv_gn_g': row(p['rwkv_gn_g']), 'rwkv_gn_b': row(p['rwkv_gn_b']),
        'w_oa': p['w_oa'][l].astype(BF16), 'w_ob': p['w_ob'][l].astype(BF16), 'w_oc': p['w_oc'][l].astype(BF16),
        'w_out': p['w_out'][l].astype(BF16),
        'norm2_g': row(p['norm2_g']),
        'w_up_a': p['w_up_ffn'][l][:, :D_FF].astype(BF16), 'w_up_b': p['w_up_ffn'][l][:, D_FF:].astype(BF16),
        'ffn_conv_w': p['ffn_conv_w'][l], 'w_down': p['w_down'][l].astype(BF16),
    }


def _pad_rows_front(a, rows):
    return jnp.pad(a, ((0, 0), (rows - a.shape[1], 0), (0, 0)))


def _state_to_pairs(s):
    B = s.shape[0]
    a = jnp.swapaxes(s, -1, -2).reshape(B, N_PAIR, 2, D_HEAD, D_HEAD)
    z = jnp.zeros_like(a[:, :, 0])
    top = jnp.concatenate([a[:, :, 0], z], axis=-1)
    bot = jnp.concatenate([z, a[:, :, 1]], axis=-1)
    return jnp.concatenate([top, bot], axis=-2)


def _pairs_to_state(a):
    B = a.shape[0]
    even = a[:, :, :D_HEAD, :D_HEAD]
    odd = a[:, :, D_HEAD:, D_HEAD:]
    s = jnp.stack([even, odd], axis=2).reshape(B, N_HEAD, D_HEAD, D_HEAD)
    return jnp.swapaxes(s, -1, -2)


def _tile(t, pref):
    return pref if t % pref == 0 else t


def _layer(x, lw, attend, conv_hist, shift_hist, wkv0, ffn_hist, norm_f_g, final, prompt):
    B, T, _ = x.shape
    if prompt:
        qc, kc, k, v, vb, logf, u, p, gates = _in_proj(x, lw, True, _tile(T, 512))
        o_a = attend(qc, kc, vb)
    else:
        flat = x.reshape(1, B * T, D_MODEL)
        q, k, v, logf, u, p, gates = [a.reshape((B, T) + a.shape[2:]) for a in _in_proj(flat, lw, False, B * T)]
        o_a = attend(q, k, v, logf)
    if T >= CONV_W - 1:
        conv_new = u[:, T - (CONV_W - 1):]
    else:
        conv_new = jnp.concatenate([conv_hist[:, T:], u], axis=1)
    o_b = _conv(u, _pad_rows_front(conv_hist, HIST_ROWS), lw, _tile(T, 512))
    shift_new = p[:, -1:]
    r, lwd, k_eff, vc, kk, beta, g, bonus = _rwkv_prep(p, _pad_rows_front(shift_hist, 8), lw, _tile(T, 512))
    chunk = 64
    nb = 2 if B % 2 == 0 else 1
    if T % chunk:
        pad = lambda a: jnp.pad(a, ((0, 0), (0, chunk - T % chunk), (0, 0)))
        y, a_new = _rwkv_scan(pad(r), pad(lwd), pad(k_eff), pad(vc), pad(kk), pad(beta), _state_to_pairs(wkv0),
                              chunk, nb)
        y = y[:, :T]
    else:
        y, a_new = _rwkv_scan(r, lwd, k_eff, vc, kk, beta, _state_to_pairs(wkv0), chunk, nb)
    wkv_new = _pairs_to_state(a_new)
    if prompt:
        x = _merge(x, o_a, o_b, y, bonus, g, gates, lw, _tile(T, 512))
    else:
        fl = lambda a: a.reshape((1, B * T) + a.shape[2:])
        x = _merge(fl(x), fl(o_a), fl(o_b), fl(y), fl(bonus), fl(g), fl(gates), lw, B * T).reshape(B, T, D_MODEL)
    if prompt or T < FFN_CONV_W - 1:
        outs = _ffn(x, _pad_rows_front(ffn_hist, 8), lw, norm_f_g, final, _tile(T, 512))
        ffn_new = outs[1][:, 8 - (FFN_CONV_W - 1):]
    else:
        outs = _ffn_short(x, ffn_hist, lw, norm_f_g, final)
        ffn_new = outs[1]
    x = outs[0]
    xn = outs[2] if final else None
    state = (k.reshape(B, T, N_HEAD, D_HEAD), v.reshape(B, T, N_HEAD, D_HEAD), logf, conv_new, shift_new,
             wkv_new, ffn_new)
    return x, xn, state


def kernel(x_prompt, x_sample, cache_k, cache_v, cache_logf, state_conv, state_shift, state_wkv, state_ffn,
           page_table, norm1_g, w_in, b_f, b_gate, w_oa, conv_w, conv_b, conv_ln_g, conv_ln_b, w_ob,
           rwkv_mu, rwkv_w0, rwkv_w_lora_up, rwkv_a0, rwkv_a_lora_up, rwkv_g_lora_up, rwkv_k_k, rwkv_k_a,
           rwkv_r_k, rwkv_gn_g, rwkv_gn_b, w_oc, w_out, norm2_g, w_up_ffn, ffn_conv_w, w_down, norm_f_g):
    p = dict(norm1_g=norm1_g, w_in=w_in, b_f=b_f, b_gate=b_gate, w_oa=w_oa, conv_w=conv_w, conv_b=conv_b,
             conv_ln_g=conv_ln_g, conv_ln_b=conv_ln_b, w_ob=w_ob, rwkv_mu=rwkv_mu, rwkv_w0=rwkv_w0,
             rwkv_w_lora_up=rwkv_w_lora_up, rwkv_a0=rwkv_a0, rwkv_a_lora_up=rwkv_a_lora_up,
             rwkv_g_lora_up=rwkv_g_lora_up, rwkv_k_k=rwkv_k_k, rwkv_k_a=rwkv_k_a, rwkv_r_k=rwkv_r_k,
             rwkv_gn_g=rwkv_gn_g, rwkv_gn_b=rwkv_gn_b, w_oc=w_oc, w_out=w_out, norm2_g=norm2_g,
             w_up_ffn=w_up_ffn, ffn_conv_w=ffn_conv_w, w_down=w_down)
    depth = w_in.shape[0]
    nf = norm_f_g.reshape(1, D_MODEL)
    weights = [_layer_weights(l, p) for l in range(depth)]

    bp, tp, _ = x_prompt.shape
    bs, ts, _ = x_sample.shape
    page = cache_k.shape[2]

    yp, p_states = x_prompt, []
    for l in range(depth):
        attend = lambda qc, kc, vt: _fox_prompt(qc, kc, vt, _tile(tp, 512), _tile(tp, 1024))
        yp, ypn, st = _layer(yp, weights[l], attend,
                             jnp.zeros((bp, CONV_W - 1, W_HEADS), F32), jnp.zeros((bp, 1, C_SHIFT), F32),
                             jnp.zeros((bp, N_HEAD, D_HEAD, D_HEAD), F32), jnp.zeros((bp, FFN_CONV_W - 1, D_FF), F32),
                             nf, l == depth - 1, True)
        p_states.append(st)

    ys, s_states = x_sample, []
    n_pages = page_table.shape[1]
    n_group = 32 if n_pages % 32 == 0 else (16 if n_pages % 16 == 0 else 1)
    n_pool = cache_k.shape[1]
    kt_pool = jnp.transpose(cache_k, (0, 1, 3, 4, 2)).reshape(depth, n_pool, W_HEADS, page)
    vt_pool = jnp.transpose(cache_v, (0, 1, 3, 4, 2)).reshape(depth, n_pool, W_HEADS, page)
    lf_pool_t = jnp.swapaxes(cache_logf, 2, 3)
    for l in range(depth):

        def attend(q, k, v, logf, l=l):
            padk = lambda a: jnp.pad(a, ((0, 0), (0, page - ts), (0, 0)))
            lf_t = jnp.pad(jnp.swapaxes(logf, 1, 2), ((0, 0), (0, 0), (0, page - ts)))
            return _fox_sample(q, padk(k), padk(v), lf_t, l, kt_pool, vt_pool, lf_pool_t, page_table, n_group)

        ys, ysn, st = _layer(ys, weights[l], attend, state_conv[l], state_shift[l], state_wkv[l], state_ffn[l],
                             nf, l == depth - 1, False)
        s_states.append(st)

    stk = lambda states, i: jnp.stack([s[i] for s in states], axis=0)
    return (ypn, ysn,
            stk(p_states, 0), stk(p_states, 1), stk(p_states, 2), stk(p_states, 3),
            stk(p_states, 4), stk(p_states, 5), stk(p_states, 6),
            stk(s_states, 0), stk(s_states, 1), stk(s_states, 2), stk(s_states, 3),
            stk(s_states, 4), stk(s_states, 5), stk(s_states, 6))
```
